```python
import jax
import jax.numpy as jnp
from jax import lax
import numpy as np

D_MODEL = 1024
BATCH = 4
SEQ = 4096
DEPTH = 4
DEC_BATCH = 128
DEC_SEQ = 4
PAST_LEN = 2048
PAGE_SIZE = 128

N_AB = (DEPTH + 1) // 2
N_C = DEPTH // 2
D_A = D_MODEL // 2
H_B = 8
HEAD_DIM = (D_MODEL // 2) // H_B
D_B = H_B * HEAD_DIM
CONV_WIDTH = 31
D_IN_AB = 2 * D_A + 3 * D_B + H_B
D_MIX_AB = D_A + D_B
D_C = D_MODEL
H_C = 8
C_HD = D_C // H_C
CHUNK = 128
Q_BLOCK = 128
D_FF = 4 * D_MODEL
POOL_NUM = 5
POOL_DEN = 4
FORGET_BIAS_MEAN = 3.0
RMS_EPS = 1e-6
LN_EPS = 1e-5
ATTN_SCALE = HEAD_DIM ** -0.5

kernel_name = 'hybrid_conformer_fox_chunkmlp_step'


def rms_norm(x, g):
    xf = x.astype(jnp.float32)
    y = xf * lax.rsqrt(jnp.mean(xf * xf, axis=-1, keepdims=True) + RMS_EPS)
    return (y * g.astype(jnp.float32)).astype(x.dtype)


def layer_norm(x, g, b):
    xf = x.astype(jnp.float32)
    xc = xf - jnp.mean(xf, axis=-1, keepdims=True)
    var = jnp.mean(xc * xc, axis=-1, keepdims=True)
    y = xc * lax.rsqrt(var + LN_EPS) * g.astype(jnp.float32) + b.astype(jnp.float32)
    return y.astype(x.dtype)


def gather_pages(pool, page_table):
    rows = pool[page_table]
    return rows.reshape(rows.shape[0], rows.shape[1] * rows.shape[2], *rows.shape[3:])


def ab_project(h, w_in, b_f, q_g, k_g):
    proj = h @ w_in
    a_in, q, k, v, fg = jnp.split(
        proj, [2 * D_A, 2 * D_A + D_B, 2 * D_A + 2 * D_B, 2 * D_A + 3 * D_B], axis=-1)
    lead = h.shape[:-1]
    q = rms_norm(q.reshape(*lead, H_B, HEAD_DIM), q_g)
    k = rms_norm(k.reshape(*lead, H_B, HEAD_DIM), k_g)
    v = v.reshape(*lead, H_B, HEAD_DIM)
    logf = jax.nn.log_sigmoid(fg.astype(jnp.float32) + b_f.astype(jnp.float32))
    return a_in, q, k, v, logf


def conformer_conv(a_in, hist, conv_w, conv_b, ln_g, ln_b):
    val, gate = jnp.split(a_in, 2, axis=-1)
    glu = val * jax.nn.sigmoid(gate)
    full = jnp.concatenate([hist.astype(glu.dtype), glu], axis=1)
    out = lax.conv_general_dilated(
        full, conv_w[:, None, :].astype(glu.dtype), window_strides=(1,), padding='VALID',
        dimension_numbers=('NWC', 'WIO', 'NWC'), feature_group_count=D_A)
    out = out + conv_b
    y = jax.nn.silu(layer_norm(out, ln_g, ln_b))
    return y, full[:, -(CONV_WIDTH - 1):]


def fox_attend(q, k, v, f_q, f_k, q_pos, k_pos):
    s = jnp.einsum('nqhd,nkhd->nhqk', q, k).astype(jnp.float32) * ATTN_SCALE
    s = s + jnp.swapaxes(f_q, 1, 2)[:, :, :, None] - jnp.swapaxes(f_k, 1, 2)[:, :, None, :]
    mask = k_pos[None, :] <= q_pos[:, None]
    s = jnp.where(mask, s, -jnp.inf)
    p = jax.nn.softmax(s, axis=-1)
    return jnp.einsum('nhqk,nkhd->nqhd', p.astype(v.dtype), v)


def fox_prompt(q, k, v, f_cum):
    n, s = q.shape[0], q.shape[1]
    nb = s // Q_BLOCK
    qb = jnp.moveaxis(q.reshape(n, nb, Q_BLOCK, H_B, HEAD_DIM), 1, 0)
    fb = jnp.moveaxis(f_cum.reshape(n, nb, Q_BLOCK, H_B), 1, 0)
    pb = jnp.arange(s, dtype=jnp.int32).reshape(nb, Q_BLOCK)
    k_pos = jnp.arange(s, dtype=jnp.int32)
    out = lax.map(lambda a: fox_attend(a[0], k, v, a[1], f_cum, a[2], k_pos), (qb, fb, pb))
    return jnp.moveaxis(out, 0, 1).reshape(n, s, H_B, HEAD_DIM)


def chunk_spatial_gate(h, w_in, ln_g, ln_b, w_s, b_s, w_out):
    n, length, _ = h.shape
    lc = min(length, CHUNK)
    z = jax.nn.gelu(h @ w_in)
    u, v = jnp.split(z, 2, axis=-1)
    v = layer_norm(v, ln_g, ln_b)
    causal = jnp.tril(jnp.ones((lc, lc), dtype=bool))
    w = jnp.where(causal, w_s[:, :lc, :lc], 0.0).astype(v.dtype)
    vc = v.reshape(n, length // lc, lc, H_C, C_HD)
    mix = jnp.einsum('hij,ncjhd->ncihd', w, vc) + jnp.swapaxes(b_s[:, :lc], 0, 1)[:, :, None]
    g = u * mix.reshape(n, length, D_C)
    return g @ w_out, v


def sq_relu_mlp(h, w_up, w_down):
    a = jax.nn.relu(h @ w_up)
    return (a * a) @ w_down


def setup_inputs(seed: int = 0) -> dict:
    key = jax.random.key(seed)
    ks = jax.random.split(key, 32)
    f32 = jnp.float32
    n_pages = PAST_LEN // PAGE_SIZE
    n_pool = (DEC_BATCH * n_pages * POOL_NUM) // POOL_DEN

    def nrm(k, shape, scale):
        return jax.random.normal(k, shape, f32) * scale

    x_prompt = nrm(ks[0], (BATCH, SEQ, D_MODEL), 1.0)
    x_sample = nrm(ks[1], (DEC_BATCH, DEC_SEQ, D_MODEL), 1.0)
    cache_k = nrm(ks[2], (N_AB, n_pool, PAGE_SIZE, H_B, HEAD_DIM), 1.0)
    cache_v = nrm(ks[3], (N_AB, n_pool, PAGE_SIZE, H_B, HEAD_DIM), 1.0)
    cache_logf = jax.nn.log_sigmoid(
        FORGET_BIAS_MEAN + jax.random.normal(ks[4], (N_AB, n_pool, PAGE_SIZE, H_B), f32))
    state_conv = nrm(ks[5], (N_AB, DEC_BATCH, CONV_WIDTH - 1, D_A), 0.5)
    page_table = jax.random.permutation(ks[6], n_pool)[: DEC_BATCH * n_pages].reshape(
        DEC_BATCH, n_pages).astype(jnp.int32)

    norm_mix_g = 1.0 + nrm(ks[7], (DEPTH, D_MODEL), 0.1)
    norm_ffn_g = 1.0 + nrm(ks[8], (DEPTH, D_MODEL), 0.1)
    w_in_ab = nrm(ks[9], (N_AB, D_MODEL, D_IN_AB), D_MODEL ** -0.5)
    b_forget = FORGET_BIAS_MEAN + nrm(ks[10], (N_AB, H_B), 0.5)
    q_norm_g = 1.0 + nrm(ks[11], (N_AB, HEAD_DIM), 0.1)
    k_norm_g = 1.0 + nrm(ks[12], (N_AB, HEAD_DIM), 0.1)
    conv_w = nrm(ks[13], (N_AB, CONV_WIDTH, D_A), CONV_WIDTH ** -0.5)
    conv_b = nrm(ks[14], (N_AB, D_A), 0.02)
    conv_ln_g = 1.0 + nrm(ks[15], (N_AB, D_A), 0.1)
    conv_ln_b = nrm(ks[16], (N_AB, D_A), 0.02)
    w_out_ab = nrm(ks[17], (N_AB, D_MIX_AB, D_MODEL), D_MIX_AB ** -0.5)
    w_in_c = nrm(ks[18], (N_C, D_MODEL, 2 * D_C), D_MODEL ** -0.5)
    sgu_ln_g = 1.0 + nrm(ks[19], (N_C, D_C), 0.1)
    sgu_ln_b = nrm(ks[20], (N_C, D_C), 0.02)
    w_spatial = nrm(ks[21], (N_C, H_C, CHUNK, CHUNK), CHUNK ** -0.5)
    b_spatial = 1.0 + nrm(ks[22], (N_C, H_C, CHUNK), 0.1)
    w_out_c = nrm(ks[23], (N_C, D_C, D_MODEL), D_C ** -0.5)
    w_ff_up = nrm(ks[24], (DEPTH, D_MODEL, D_FF), D_MODEL ** -0.5)
    w_ff_down = nrm(ks[25], (DEPTH, D_FF, D_MODEL), D_FF ** -0.5)
    return {
        'x_prompt': x_prompt, 'x_sample': x_sample,
        'cache_k': cache_k, 'cache_v': cache_v, 'cache_logf': cache_logf,
        'state_conv': state_conv, 'page_table': page_table,
        'norm_mix_g': norm_mix_g, 'norm_ffn_g': norm_ffn_g,
        'w_in_ab': w_in_ab, 'b_forget': b_forget, 'q_norm_g': q_norm_g, 'k_norm_g': k_norm_g,
        'conv_w': conv_w, 'conv_b': conv_b, 'conv_ln_g': conv_ln_g, 'conv_ln_b': conv_ln_b,
        'w_out_ab': w_out_ab,
        'w_in_c': w_in_c, 'sgu_ln_g': sgu_ln_g, 'sgu_ln_b': sgu_ln_b,
        'w_spatial': w_spatial, 'b_spatial': b_spatial, 'w_out_c': w_out_c,
        'w_ff_up': w_ff_up, 'w_ff_down': w_ff_down,
    }


def reference(x_prompt, x_sample, cache_k, cache_v, cache_logf, state_conv, page_table,
              norm_mix_g, norm_ffn_g, w_in_ab, b_forget, q_norm_g, k_norm_g,
              conv_w, conv_b, conv_ln_g, conv_ln_b, w_out_ab,
              w_in_c, sgu_ln_g, sgu_ln_b, w_spatial, b_spatial, w_out_c,
              w_ff_up, w_ff_down):
    past_len = page_table.shape[1] * cache_k.shape[2]
    dec_seq = x_sample.shape[1]
    q_pos_s = past_len + jnp.arange(dec_seq, dtype=jnp.int32)
    k_pos_s = jnp.arange(past_len + dec_seq, dtype=jnp.int32)

    xp, xs = x_prompt, x_sample
    kp_l, vp_l, fp_l, cp_l = [], [], [], []
    ks_l, vs_l, fs_l, cs_l = [], [], [], []
    chs_l = []

    for layer in range(DEPTH):
        hp = rms_norm(xp, norm_mix_g[layer])
        hs = rms_norm(xs, norm_mix_g[layer])
        if layer % 2 == 0:
            e = layer // 2
            wi, bf, qg, kg = w_in_ab[e], b_forget[e], q_norm_g[e], k_norm_g[e]
            cw, cb, lg, lb = conv_w[e], conv_b[e], conv_ln_g[e], conv_ln_b[e]
            wo = w_out_ab[e]

            a_in, q, k, v, logf = ab_project(hp, wi, bf, qg, kg)
            zero_hist = jnp.zeros((xp.shape[0], CONV_WIDTH - 1, D_A), xp.dtype)
            a_out, hist_p = conformer_conv(a_in, zero_hist, cw, cb, lg, lb)
            att = fox_prompt(q, k, v, jnp.cumsum(logf, axis=1))
            merged = jnp.concatenate([a_out, att.reshape(*att.shape[:2], D_B)], axis=-1)
            xp = xp + merged @ wo
            kp_l.append(k)
            vp_l.append(v)
            fp_l.append(logf)
            cp_l.append(hist_p)

            a_in, q, k, v, logf = ab_project(hs, wi, bf, qg, kg)
            a_out, hist_s = conformer_conv(a_in, state_conv[e], cw, cb, lg, lb)
            k_all = jnp.concatenate([gather_pages(cache_k[e], page_table).astype(k.dtype), k], axis=1)
            v_all = jnp.concatenate([gather_pages(cache_v[e], page_table).astype(v.dtype), v], axis=1)
            f_all = jnp.cumsum(jnp.concatenate(
                [gather_pages(cache_logf[e], page_table).astype(jnp.float32), logf], axis=1), axis=1)
            att = fox_attend(q, k_all, v_all, f_all[:, past_len:], f_all, q_pos_s, k_pos_s)
            merged = jnp.concatenate([a_out, att.reshape(*att.shape[:2], D_B)], axis=-1)
            xs = xs + merged @ wo
            ks_l.append(k)
            vs_l.append(v)
            fs_l.append(logf)
            cs_l.append(hist_s)
        else:
            o = layer // 2
            mix_p, _ = chunk_spatial_gate(hp, w_in_c[o], sgu_ln_g[o], sgu_ln_b[o],
                                          w_spatial[o], b_spatial[o], w_out_c[o])
            xp = xp + mix_p
            mix_s, v_rows = chunk_spatial_gate(hs, w_in_c[o], sgu_ln_g[o], sgu_ln_b[o],
                                               w_spatial[o], b_spatial[o], w_out_c[o])
            xs = xs + mix_s
            chs_l.append(v_rows)
        xp = xp + sq_relu_mlp(rms_norm(xp, norm_ffn_g[layer]), w_ff_up[layer], w_ff_down[layer])
        xs = xs + sq_relu_mlp(rms_norm(xs, norm_ffn_g[layer]), w_ff_up[layer], w_ff_down[layer])

    return (xp, xs,
            jnp.stack(kp_l), jnp.stack(vp_l), jnp.stack(fp_l),
            jnp.stack(ks_l), jnp.stack(vs_l), jnp.stack(fs_l),
            jnp.stack(cp_l), jnp.stack(cs_l),
            jnp.stack(chs_l))
```

```python
import functools

import jax
import jax.numpy as jnp
from jax import lax
from jax.experimental import pallas as pl
from jax.experimental.pallas import tpu as pltpu

BF = jnp.bfloat16
F32 = jnp.float32

TM = 512
LANES = 128
SUBLANES = 8
N_HEADS = 8
HEAD_DIM = 64
PAIR = 2 * HEAD_DIM
CONV_WIDTH = 31
HIST = CONV_WIDTH - 1
HIST_PAD = 32
CHUNK = 128
RMS_EPS = 1e-6
LN_EPS = 1e-5
NEG = -1e30
VMEM_LIMIT = 56 * 1024 * 1024


def _dot(a, b):
    return jnp.dot(a, b, preferred_element_type=F32)


def _dot_nt(a, b):
    return lax.dot_general(a, b, (((1,), (1,)), ((), ())), preferred_element_type=F32)


def _rms(x, g):
    ms = jnp.mean(x * x, axis=-1, keepdims=True)
    return x * lax.rsqrt(ms + RMS_EPS) * g


def _ln(x, g, b):
    mu = jnp.mean(x, axis=-1, keepdims=True)
    xc = x - mu
    var = jnp.mean(xc * xc, axis=-1, keepdims=True)
    return xc * lax.rsqrt(var + LN_EPS) * g + b


def _log_sigmoid(z):
    return jnp.minimum(z, 0.0) - jnp.log1p(jnp.exp(-jnp.abs(z)))


def _split3(x):
    hi = x.astype(BF)
    r1 = x - hi.astype(F32)
    mid = r1.astype(BF)
    lo = (r1 - mid.astype(F32)).astype(BF)
    return hi, mid, lo


def _ffn(x, g_ref, wu_ref, wd_ref, chunk=1024):
    hn = _rms(x, g_ref[...]).astype(BF)
    acc = None
    for c in range(wu_ref.shape[1] // chunk):
        a = _dot(hn, wu_ref[:, c * chunk:(c + 1) * chunk])
        a = jnp.maximum(a, 0.0)
        part = _dot((a * a).astype(BF), wd_ref[c * chunk:(c + 1) * chunk, :])
        acc = part if acc is None else acc + part
    return x + acc


def _resident(shape):
    nd = len(shape)
    return pl.BlockSpec(shape, lambda *_: (0,) * nd, pipeline_mode=pl.Buffered(1))


def _even_in_kernel(xp_ref, xs_ref, g_ref, w_ref, bfg_ref, qg_ref, kg_ref, bd_ref,
                    cw_ref, cb_ref, lg_ref, lb_ref,
                    ap_ref, qpm_ref, kpm_ref, vpm_ref, kp_ref, vp_ref, lfp_ref, fp_ref,
                    histp_ref, qs_ref, ks_ref, vs_ref, glus_ref, lfs_ref,
                    gbuf_ref, fcarry_ref, *, n_ptiles, tiles_per_seq):
    i = pl.program_id(0)
    is_s = i >= n_ptiles
    is_p = jnp.logical_not(is_s)
    seq_first = jnp.logical_and(is_p, i % tiles_per_seq == 0)
    seq_last = jnp.logical_and(is_p, i % tiles_per_seq == tiles_per_seq - 1)
    d_a = cw_ref.shape[1]
    d_b = N_HEADS * HEAD_DIM

    x = jnp.where(is_s, xs_ref[...], xp_ref[...])
    h = _rms(x, g_ref[...]).astype(BF)

    a_in = _dot(h, w_ref[:, 0:2 * d_a])
    glu = a_in[:, :d_a] * jax.nn.sigmoid(a_in[:, d_a:])
    c0 = 2 * d_a
    q = _dot(h, w_ref[:, c0:c0 + d_b])
    k = _dot(h, w_ref[:, c0 + d_b:c0 + 2 * d_b])
    v = _dot(h, w_ref[:, c0 + 2 * d_b:c0 + 3 * d_b])
    fg = _dot(h, w_ref[:, c0 + 3 * d_b:c0 + 3 * d_b + LANES])

    bd = bd_ref[...]
    qn = q * lax.rsqrt(_dot((q * q).astype(BF), bd) + RMS_EPS) * qg_ref[...]
    kn = k * lax.rsqrt(_dot((k * k).astype(BF), bd) + RMS_EPS) * kg_ref[...]
    logf = _log_sigmoid(fg + bfg_ref[...])

    q_b = (qn * (HEAD_DIM ** -0.5)).astype(BF)
    k_b = kn.astype(BF)
    v_b = v.astype(BF)
    for hp in range(d_b // PAIR):
        sl = slice(hp * PAIR, (hp + 1) * PAIR)
        qpm_ref[hp] = q_b[:, sl]
        kpm_ref[hp] = k_b[:, sl]
        vpm_ref[hp] = v_b[:, sl]

    @pl.when(is_s)
    def _():
        qs_ref[...] = q_b
        ks_ref[...] = kn
        vs_ref[...] = v
        glus_ref[...] = glu
        lfs_ref[...] = logf

    @pl.when(seq_first)
    def _():
        gbuf_ref[0:HIST_PAD, :] = jnp.zeros((HIST_PAD, d_a), F32)
        fcarry_ref[...] = jnp.zeros_like(fcarry_ref)

    @pl.when(is_p)
    def _():
        kp_ref[...] = kn
        vp_ref[...] = v
        lfp_ref[...] = logf

        r = lax.broadcasted_iota(jnp.int32, (CHUNK, CHUNK), 0)
        c = lax.broadcasted_iota(jnp.int32, (CHUNK, CHUNK), 1)
        tril = (r >= c).astype(F32).astype(BF)
        hi, mid, lo = _split3(logf)
        carry = fcarry_ref[...]
        for cb in range(TM // CHUNK):
            sl = slice(cb * CHUNK, (cb + 1) * CHUNK)
            fc = _dot(tril, hi[sl]) + _dot(tril, mid[sl]) + _dot(tril, lo[sl]) + carry
            fp_ref[sl, :] = fc
            carry = fc[CHUNK - 1:CHUNK, :]
        fcarry_ref[...] = carry

        gbuf_ref[HIST_PAD:HIST_PAD + TM, :] = glu
        acc = jnp.zeros((TM, d_a), F32)
        for t in range(CONV_WIDTH):
            acc = acc + cw_ref[t:t + 1, :] * gbuf_ref[pl.ds(HIST_PAD - HIST + t, TM), :]
        y = _ln(acc + cb_ref[...], lg_ref[...], lb_ref[...])
        ap_ref[...] = (y * jax.nn.sigmoid(y)).astype(BF)
        gbuf_ref[0:HIST_PAD, :] = gbuf_ref[TM:TM + HIST_PAD, :]

    @pl.when(seq_last)
    def _():
        histp_ref[...] = gbuf_ref[pl.ds(HIST_PAD - HIST, HIST), :]


def _even_in(xp, xs, g, w, bfg, qg, kg, bd, cw, cb, lg, lb, *, batch, seq):
    n_p, d = xp.shape
    d_a = cw.shape[1]
    d_b = N_HEADS * HEAD_DIM
    n_ptiles = n_p // TM
    tiles_per_seq = seq // TM
    n_tot = n_p + TM
    last_p = n_ptiles - 1
    p_row = lambda i: (jnp.minimum(i, last_p), 0)
    s_row = lambda i: (0, 0)
    pm_spec = pl.BlockSpec((d_b // PAIR, TM, PAIR), lambda i: (0, i, 0))
    kernel = functools.partial(_even_in_kernel, n_ptiles=n_ptiles, tiles_per_seq=tiles_per_seq)
    out_shape = (
        jax.ShapeDtypeStruct((n_p, d_a), BF),
        jax.ShapeDtypeStruct((d_b // PAIR, n_tot, PAIR), BF),
        jax.ShapeDtypeStruct((d_b // PAIR, n_tot, PAIR), BF),
        jax.ShapeDtypeStruct((d_b // PAIR, n_tot, PAIR), BF),
        jax.ShapeDtypeStruct((n_p, d_b), F32),
        jax.ShapeDtypeStruct((n_p, d_b), F32),
        jax.ShapeDtypeStruct((n_p, LANES), F32),
        jax.ShapeDtypeStruct((n_p, LANES), F32),
        jax.ShapeDtypeStruct((batch, HIST, d_a), F32),
        jax.ShapeDtypeStruct((TM, d_b), BF),
        jax.ShapeDtypeStruct((TM, d_b), F32),
        jax.ShapeDtypeStruct((TM, d_b), F32),
        jax.ShapeDtypeStruct((TM, d_a), F32),
        jax.ShapeDtypeStruct((TM, LANES), F32),
    )
    out_specs = (
        pl.BlockSpec((TM, d_a), p_row),
        pm_spec, pm_spec, pm_spec,
        pl.BlockSpec((TM, d_b), p_row),
        pl.BlockSpec((TM, d_b), p_row),
        pl.BlockSpec((TM, LANES), p_row),
        pl.BlockSpec((TM, LANES), p_row),
        pl.BlockSpec((None, HIST, d_a),
                     lambda i: (jnp.minimum(i // tiles_per_seq, batch - 1), 0, 0)),
        pl.BlockSpec((TM, d_b), s_row),
        pl.BlockSpec((TM, d_b), s_row),
        pl.BlockSpec((TM, d_b), s_row),
        pl.BlockSpec((TM, d_a), s_row),
        pl.BlockSpec((TM, LANES), s_row),
    )
    in_specs = [
        pl.BlockSpec((TM, d), p_row),
        _resident(xs.shape),
        _resident(g.shape), _resident(w.shape), _resident(bfg.shape),
        _resident(qg.shape), _resident(kg.shape), _resident(bd.shape),
        _resident(cw.shape), _resident(cb.shape), _resident(lg.shape), _resident(lb.shape),
    ]
    return pl.pallas_call(
        kernel,
        grid=(n_ptiles + 1,),
        in_specs=in_specs,
        out_specs=out_specs,
        out_shape=out_shape,
        scratch_shapes=[pltpu.VMEM((HIST_PAD + TM, d_a), F32),
                        pltpu.VMEM((1, LANES), F32)],
        compiler_params=pltpu.CompilerParams(
            dimension_semantics=("arbitrary",), vmem_limit_bytes=VMEM_LIMIT),
        name="even_in",
    )(xp, xs, g, w, bfg, qg, kg, bd, cw, cb, lg, lb)


def _sample_conv_kernel(glu_ref, hist_ref, cw_ref, cb_ref, lg_ref, lb_ref, a_ref,
                        *, dec_batch, dec_seq):
    d_a = cw_ref.shape[1]

    def full(j):
        if j < HIST:
            return hist_ref[:, j * d_a:(j + 1) * d_a]
        return glu_ref[(j - HIST) * dec_batch:(j - HIST + 1) * dec_batch, :]

    for t in range(dec_seq):
        acc = jnp.zeros((dec_batch, d_a), F32)
        for kk in range(CONV_WIDTH):
            acc = acc + cw_ref[kk:kk + 1, :] * full(t + kk)
        y = _ln(acc + cb_ref[...], lg_ref[...], lb_ref[...])
        a_ref[t * dec_batch:(t + 1) * dec_batch, :] = (y * jax.nn.sigmoid(y)).astype(BF)


def _sample_conv(glu_s, hist2d, cw, cb, lg, lb, *, dec_batch, dec_seq):
    kernel = functools.partial(_sample_conv_kernel, dec_batch=dec_batch, dec_seq=dec_seq)
    return pl.pallas_call(
        kernel,
        out_shape=jax.ShapeDtypeStruct(glu_s.shape, BF),
        compiler_params=pltpu.CompilerParams(vmem_limit_bytes=VMEM_LIMIT),
        name="sample_conv",
    )(glu_s, hist2d, cw, cb, lg, lb)


def _flash_kernel(q_ref, k_ref, v_ref, f0_ref, f1_ref, o_ref, *, tq):
    qi = pl.program_id(2)
    q2 = q_ref[...]
    lane = lax.broadcasted_iota(jnp.int32, (1, PAIR), 1)
    first = lane < HEAD_DIM
    zero = jnp.zeros_like(q2)
    qh = (jnp.where(first, q2, zero), jnp.where(first, zero, q2))
    f_refs = (f0_ref, f1_ref)
    q0 = pl.multiple_of(qi * tq, tq)
    f_base = tuple(f[:, pl.ds(q0, LANES)][:, 0:1] for f in f_refs)

    def tile(start, carry, masked):
        m, l, acc = carry
        k2 = k_ref[pl.ds(start, tq), :]
        v2 = v_ref[pl.ds(start, tq), :]
        new_m, new_l, alphas, pvs = [], [], [], []
        for hh in range(2):
            s = _dot_nt(qh[hh], k2) - (f_refs[hh][:, pl.ds(start, tq)] - f_base[hh])
            if masked:
                r = lax.broadcasted_iota(jnp.int32, (tq, tq), 0)
                c = lax.broadcasted_iota(jnp.int32, (tq, tq), 1)
                s = jnp.where(r >= c, s, NEG)
            m_new = jnp.maximum(m[hh], jnp.max(s, axis=1, keepdims=True))
            alpha = jnp.exp(m[hh] - m_new)
            p = jnp.exp(s - m_new)
            new_l.append(alpha * l[hh] + jnp.sum(p, axis=1, keepdims=True))
            new_m.append(m_new)
            alphas.append(alpha)
            pvs.append(_dot(p.astype(BF), v2))
        acc = acc * jnp.where(first, alphas[0], alphas[1]) + jnp.where(first, pvs[0], pvs[1])
        return tuple(new_m), tuple(new_l), acc

    col = jnp.full((tq, 1), NEG, F32)
    zcol = jnp.zeros((tq, 1), F32)
    init = ((col, col), (zcol, zcol), jnp.zeros((tq, PAIR), F32))

    def body(ki, carry):
        return tile(pl.multiple_of(ki * tq, tq), carry, False)

    carry = lax.fori_loop(0, qi, body, init)
    _, l, acc = tile(q0, carry, True)
    o_ref[...] = (acc / jnp.where(first, l[0], l[1])).astype(BF)


def _flash(q_pm, k_pm, v_pm, f_t, *, batch, seq):
    n_pairs = q_pm.shape[0]
    tq = TM
    nq = seq // tq
    kernel = functools.partial(_flash_kernel, tq=tq)
    kv_spec = pl.BlockSpec((None, seq, PAIR), lambda b, hp, qi: (hp, b, 0))
    return pl.pallas_call(
        kernel,
        grid=(batch, n_pairs, nq),
        in_specs=[
            pl.BlockSpec((None, tq, PAIR), lambda b, hp, qi: (hp, b * nq + qi, 0)),
            kv_spec, kv_spec,
            pl.BlockSpec((None, 1, seq), lambda b, hp, qi: (2 * hp, 0, b)),
            pl.BlockSpec((None, 1, seq), lambda b, hp, qi: (2 * hp + 1, 0, b)),
        ],
        out_specs=pl.BlockSpec((tq, PAIR), lambda b, hp, qi: (b * nq + qi, hp)),
        out_shape=jax.ShapeDtypeStruct((batch * seq, n_pairs * PAIR), BF),
        compiler_params=pltpu.CompilerParams(
            dimension_semantics=("arbitrary", "arbitrary", "arbitrary"),
            vmem_limit_bytes=VMEM_LIMIT),
        name="flash_prompt",
    )(q_pm, k_pm, v_pm, f_t, f_t)


def _decode_kernel(pt_ref, q_ref, kn_ref, vn_ref, lfn_ref, *rest, n_pages, dec_seq, page):
    del pt_ref
    k_refs = rest[:n_pages]
    v_refs = rest[n_pages:2 * n_pages]
    lf_refs = rest[2 * n_pages:3 * n_pages]
    o_ref = rest[3 * n_pages]
    kx_ref, vx_ref = rest[3 * n_pages + 1:]
    d_b = N_HEADS * HEAD_DIM
    rows = dec_seq * N_HEADS

    @pl.when(pl.program_id(0) == 0)
    def _():
        kx_ref[...] = jnp.zeros_like(kx_ref)
        vx_ref[...] = jnp.zeros_like(vx_ref)

    kx_ref[0:dec_seq, :] = kn_ref[:, 0, :]
    vx_ref[0:dec_seq, :] = vn_ref[:, 0, :]

    lane_h = lax.broadcasted_iota(jnp.int32, (N_HEADS, d_b), 1) // HEAD_DIM
    row_h = lax.broadcasted_iota(jnp.int32, (N_HEADS, d_b), 0)
    head_mask = lane_h == row_h
    q4 = q_ref[:, 0, :].astype(F32)
    q32 = jnp.where(head_mask[None], q4[:, None, :], 0.0).reshape(rows, d_b).astype(BF)

    lf = jnp.concatenate([r[...] for r in lf_refs], axis=0)
    lane = lax.broadcasted_iota(jnp.int32, lf.shape, 1)
    sfx = lf
    d = 1
    while d < page:
        sfx = sfx + jnp.where(lane < page - d, pltpu.roll(sfx, page - d, axis=1), 0.0)
        d *= 2
    excl = sfx - lf
    tot = sfx[:, 0:1]
    bias_pages = [None] * n_pages
    run = jnp.zeros((N_HEADS, 1), F32)
    for p in range(n_pages - 1, -1, -1):
        sl = slice(p * N_HEADS, (p + 1) * N_HEADS)
        bias_pages[p] = excl[sl] + run
        run = run + tot[sl]

    s_pages = []
    for p in range(n_pages):
        s = _dot_nt(q32, k_refs[p][...].astype(BF)).reshape(dec_seq, N_HEADS, page)
        s_pages.append(s + bias_pages[p][None])

    lfn = lfn_ref[...]
    lane8 = lax.broadcasted_iota(jnp.int32, lfn.shape, 1)
    cs = lfn
    d = 1
    while d < dec_seq:
        cs = cs + jnp.where(lane8 >= d, pltpu.roll(cs, d, axis=1), 0.0)
        d *= 2
    s_new = _dot_nt(q32, kx_ref[...].astype(BF)).reshape(dec_seq, N_HEADS, page) - cs[None]
    step = lax.broadcasted_iota(jnp.int32, (dec_seq, N_HEADS, page), 0)
    key = lax.broadcasted_iota(jnp.int32, (dec_seq, N_HEADS, page), 2)
    s_new = jnp.where(key <= step, s_new, NEG)

    m = jnp.max(s_new, axis=2, keepdims=True)
    for s in s_pages:
        m = jnp.maximum(m, jnp.max(s, axis=2, keepdims=True))
    p_new = jnp.exp(s_new - m)
    l = jnp.sum(p_new, axis=2, keepdims=True)
    o = _dot(p_new.reshape(rows, page).astype(BF), vx_ref[...].astype(BF))
    for p in range(n_pages):
        pp = jnp.exp(s_pages[p] - m)
        l = l + jnp.sum(pp, axis=2, keepdims=True)
        o = o + _dot(pp.reshape(rows, page).astype(BF), v_refs[p][...].astype(BF))
    o = o.reshape(dec_seq, N_HEADS, d_b) / l
    o_ref[:, 0, :] = jnp.sum(jnp.where(head_mask[None], o, 0.0), axis=1).astype(BF)


def _decode(page_table_flat, q_s, k_s, v_s, lfn_t, cache_k, cache_v, cache_lf_t, *,
            layer, dec_batch, dec_seq):
    n_pages = page_table_flat.shape[0] // dec_batch
    page = cache_k.shape[2]
    d_b = cache_k.shape[3]
    tok_spec = pl.BlockSpec((dec_seq, None, 1, d_b), lambda b, pt: (0, b, 0, 0))

    def page_map(p, b, pt):
        return (layer, pt[b * n_pages + p], 0, 0)

    kv_specs = [pl.BlockSpec((None, None, page, d_b), functools.partial(page_map, p))
                for p in range(n_pages)]
    lf_specs = [pl.BlockSpec((None, None, N_HEADS, page), functools.partial(page_map, p))
                for p in range(n_pages)]
    kernel = functools.partial(_decode_kernel, n_pages=n_pages, dec_seq=dec_seq, page=page)
    grid_spec = pltpu.PrefetchScalarGridSpec(
        num_scalar_prefetch=1,
        grid=(dec_batch,),
        in_specs=[tok_spec, tok_spec, tok_spec,
                  pl.BlockSpec((None, N_HEADS, page), lambda b, pt: (b, 0, 0))]
        + kv_specs + kv_specs + lf_specs,
        out_specs=tok_spec,
        scratch_shapes=[pltpu.VMEM((page, d_b), F32), pltpu.VMEM((page, d_b), F32)],
    )
    return pl.pallas_call(
        kernel,
        grid_spec=grid_spec,
        out_shape=jax.ShapeDtypeStruct((dec_seq, dec_batch, 1, d_b), BF),
        compiler_params=pltpu.CompilerParams(
            dimension_semantics=("arbitrary",), vmem_limit_bytes=VMEM_LIMIT),
        name="decode_attn",
    )(page_table_flat, q_s, k_s, v_s, lfn_t,
      *([cache_k] * n_pages), *([cache_v] * n_pages), *([cache_lf_t] * n_pages))


def _even_out_kernel(xp_ref, xs_ref, ap_ref, as_ref, tp_ref, ts_ref, woa_ref, wob_ref,
                     g_ref, wu_ref, wd_ref, yp_ref, ys_ref, *, n_ptiles):
    i = pl.program_id(0)
    is_s = i >= n_ptiles
    x = jnp.where(is_s, xs_ref[...], xp_ref[...])
    a = jnp.where(is_s, as_ref[...], ap_ref[...])
    t = jnp.where(is_s, ts_ref[...], tp_ref[...])
    x = x + (_dot(a, woa_ref[...]) + _dot(t, wob_ref[...]))
    y = _ffn(x, g_ref, wu_ref, wd_ref)

    @pl.when(is_s)
    def _():
        ys_ref[...] = y

    @pl.when(jnp.logical_not(is_s))
    def _():
        yp_ref[...] = y


def _even_out(xp, xs, a_p, a_s, t_p, t_s, woa, wob, g, wu, wd):
    n_p, d = xp.shape
    n_ptiles = n_p // TM
    last_p = n_ptiles - 1
    p_row = lambda i: (jnp.minimum(i, last_p), 0)
    kernel = functools.partial(_even_out_kernel, n_ptiles=n_ptiles)
    return pl.pallas_call(
        kernel,
        grid=(n_ptiles + 1,),
        in_specs=[
            pl.BlockSpec((TM, d), p_row), _resident(xs.shape),
            pl.BlockSpec((TM, a_p.shape[1]), p_row), _resident(a_s.shape),
            pl.BlockSpec((TM, t_p.shape[1]), p_row), _resident(t_s.shape),
            _resident(woa.shape), _resident(wob.shape),
            _resident(g.shape), _resident(wu.shape), _resident(wd.shape),
        ],
        out_specs=(pl.BlockSpec((TM, d), p_row), pl.BlockSpec((TM, d), lambda i: (0, 0))),
        out_shape=(jax.ShapeDtypeStruct(xp.shape, F32), jax.ShapeDtypeStruct(xs.shape, F32)),
        compiler_params=pltpu.CompilerParams(
            dimension_semantics=("arbitrary",), vmem_limit_bytes=VMEM_LIMIT),
        name="even_out",
    )(xp, xs, a_p, a_s, t_p, t_s, woa, wob, g, wu, wd)


def _odd_kernel(xp_ref, xs_ref, gm_ref, wi_ref, lg_ref, lb_ref, ws_ref, bs_ref, wsm_ref,
                wo_ref, gf_ref, wu_ref, wd_ref, yp_ref, ys_ref, vs_ref, mix_ref,
                *, n_ptiles, dec_batch, dec_seq):
    i = pl.program_id(0)
    is_s = i >= n_ptiles
    d_c = wo_ref.shape[0]
    n_groups = ws_ref.shape[0]
    c_hd = d_c // n_groups

    x = jnp.where(is_s, xs_ref[...], xp_ref[...])
    h = _rms(x, gm_ref[...]).astype(BF)
    u = jax.nn.gelu(_dot(h, wi_ref[:, :d_c]))
    v = _ln(jax.nn.gelu(_dot(h, wi_ref[:, d_c:])), lg_ref[...], lb_ref[...])
    v_b = v.astype(BF)

    @pl.when(jnp.logical_not(is_s))
    def _():
        r = lax.broadcasted_iota(jnp.int32, (CHUNK, CHUNK), 0)
        c = lax.broadcasted_iota(jnp.int32, (CHUNK, CHUNK), 1)
        for hg in range(n_groups):
            w = jnp.where(r >= c, ws_ref[hg], jnp.zeros((CHUNK, CHUNK), BF))
            cols = slice(hg * c_hd, (hg + 1) * c_hd)
            for cb in range(TM // CHUNK):
                rows = slice(cb * CHUNK, (cb + 1) * CHUNK)
                mix_ref[rows, cols] = _dot(w, v_b[rows, cols]) + bs_ref[:, cols]

    @pl.when(is_s)
    def _():
        vs_ref[...] = v
        v_r = v_b.astype(F32)
        for t in range(dec_seq):
            acc = jnp.zeros((dec_batch, d_c), F32) + bs_ref[t:t + 1, :]
            for j in range(t + 1):
                wt = wsm_ref[t * dec_seq + j:t * dec_seq + j + 1, :].astype(BF).astype(F32)
                acc = acc + wt * v_r[j * dec_batch:(j + 1) * dec_batch, :]
            mix_ref[t * dec_batch:(t + 1) * dec_batch, :] = acc

    gte = (u * mix_ref[...]).astype(BF)
    x = x + _dot(gte, wo_ref[...])
    y = _ffn(x, gf_ref, wu_ref, wd_ref)

    @pl.when(is_s)
    def _():
        ys_ref[...] = y

    @pl.when(jnp.logical_not(is_s))
    def _():
        yp_ref[...] = y


def _odd(xp, xs, gm, wi, lg, lb, ws, bs_full, ws_small, wo, gf, wu, wd, *, dec_batch, dec_seq):
    n_p, d = xp.shape
    d_c = wo.shape[0]
    n_ptiles = n_p // TM
    last_p = n_ptiles - 1
    p_row = lambda i: (jnp.minimum(i, last_p), 0)
    s_row = lambda i: (0, 0)
    kernel = functools.partial(_odd_kernel, n_ptiles=n_ptiles, dec_batch=dec_batch,
                               dec_seq=dec_seq)
    return pl.pallas_call(
        kernel,
        grid=(n_ptiles + 1,),
        in_specs=[
            pl.BlockSpec((TM, d), p_row), _resident(xs.shape),
            _resident(gm.shape), _resident(wi.shape), _resident(lg.shape), _resident(lb.shape),
            _resident(ws.shape), _resident(bs_full.shape), _resident(ws_small.shape),
            _resident(wo.shape), _resident(gf.shape), _resident(wu.shape), _resident(wd.shape),
        ],
        out_specs=(pl.BlockSpec((TM, d), p_row), pl.BlockSpec((TM, d), s_row),
                   pl.BlockSpec((TM, d_c), s_row)),
        out_shape=(jax.ShapeDtypeStruct(xp.shape, F32), jax.ShapeDtypeStruct(xs.shape, F32),
                   jax.ShapeDtypeStruct((TM, d_c), F32)),
        scratch_shapes=[pltpu.VMEM((TM, d_c), F32)],
        compiler_params=pltpu.CompilerParams(
            dimension_semantics=("arbitrary",), vmem_limit_bytes=VMEM_LIMIT),
        name="odd_layer",
    )(xp, xs, gm, wi, lg, lb, ws, bs_full, ws_small, wo, gf, wu, wd)


def kernel(x_prompt, x_sample, cache_k, cache_v, cache_logf, state_conv, page_table,
           norm_mix_g, norm_ffn_g, w_in_ab, b_forget, q_norm_g, k_norm_g,
           conv_w, conv_b, conv_ln_g, conv_ln_b, w_out_ab,
           w_in_c, sgu_ln_g, sgu_ln_b, w_spatial, b_spatial, w_out_c,
           w_ff_up, w_ff_down):
    batch, seq, d = x_prompt.shape
    dec_batch, dec_seq, _ = x_sample.shape
    depth = norm_mix_g.shape[0]
    n_ab, n_pool, page, n_heads, head_dim = cache_k.shape
    d_b = n_heads * head_dim
    d_a = conv_w.shape[2]
    n_groups = w_spatial.shape[1]
    d_c = w_out_c.shape[1]
    assert dec_batch * dec_seq == TM and seq % TM == 0 and page == LANES
    assert n_heads == N_HEADS and head_dim == HEAD_DIM and conv_w.shape[1] == CONV_WIDTH
    assert w_spatial.shape[2] == CHUNK and dec_seq <= SUBLANES

    xp = x_prompt.reshape(batch * seq, d)
    xs = jnp.swapaxes(x_sample, 0, 1).reshape(TM, d)

    row = lambda a: a.reshape(1, -1).astype(F32)
    ck = cache_k.reshape(n_ab, n_pool, page, d_b)
    cv = cache_v.reshape(n_ab, n_pool, page, d_b)
    clf_t = jnp.swapaxes(cache_logf, 2, 3)
    pt_flat = page_table.reshape(-1)
    hist2d = state_conv.reshape(n_ab, dec_batch, HIST * d_a)
    head_of = jnp.arange(d_b) // head_dim
    bd = (head_of[:, None] == head_of[None, :]).astype(BF) * (1.0 / head_dim)

    outs = {k: [] for k in ("kp", "vp", "fp", "cp", "ks", "vs", "fs", "cs", "chs")}
    for layer in range(depth):
        g_mix = row(norm_mix_g[layer])
        g_ffn = row(norm_ffn_g[layer])
        wu = w_ff_up[layer].astype(BF)
        wd = w_ff_down[layer].astype(BF)
        if layer % 2 == 0:
            e = layer // 2
            w_in = jnp.pad(w_in_ab[e], ((0, 0), (0, LANES - n_heads))).astype(BF)
            bfg = jnp.pad(b_forget[e], (0, LANES - n_heads)).reshape(1, LANES)
            qg = row(jnp.tile(q_norm_g[e], n_heads))
            kg = row(jnp.tile(k_norm_g[e], n_heads))
            cw, cb = conv_w[e], row(conv_b[e])
            lg, lb = row(conv_ln_g[e]), row(conv_ln_b[e])
            (a_p, q_pm, k_pm, v_pm, k_p, v_p, lf_p, f_p, hist_p,
             q_s, k_s, v_s, glu_s, lf_s) = _even_in(
                xp, xs, g_mix, w_in, bfg, qg, kg, bd, cw, cb, lg, lb, batch=batch, seq=seq)

            a_s = _sample_conv(glu_s, hist2d[e], cw, cb, lg, lb,
                               dec_batch=dec_batch, dec_seq=dec_seq)

            f_t = f_p[:, :n_heads].T.reshape(n_heads, 1, batch * seq)
            t_p = _flash(q_pm, k_pm, v_pm, f_t, batch=batch, seq=seq)

            lf_s4 = lf_s[:, :n_heads].reshape(dec_seq, dec_batch, n_heads)
            lfn_t = jnp.pad(jnp.transpose(lf_s4, (1, 2, 0)),
                            ((0, 0), (0, 0), (0, page - dec_seq)))
            tok = lambda a: a.reshape(dec_seq, dec_batch, 1, d_b)
            t_s = _decode(pt_flat, tok(q_s), tok(k_s), tok(v_s), lfn_t, ck, cv, clf_t,
                          layer=e, dec_batch=dec_batch, dec_seq=dec_seq).reshape(TM, d_b)

            wo = w_out_ab[e].astype(BF)
            xp, xs = _even_out(xp, xs, a_p, a_s, t_p, t_s, wo[:d_a], wo[d_a:], g_ffn, wu, wd)

            sample_major = lambda a: jnp.swapaxes(a.reshape(dec_seq, dec_batch, -1), 0, 1)
            outs["kp"].append(k_p.reshape(batch, seq, n_heads, head_dim))
            outs["vp"].append(v_p.reshape(batch, seq, n_heads, head_dim))
            outs["fp"].append(lf_p[:, :n_heads].reshape(batch, seq, n_heads))
            outs["cp"].append(hist_p)
            outs["ks"].append(sample_major(k_s).reshape(dec_batch, dec_seq, n_heads, head_dim))
            outs["vs"].append(sample_major(v_s).reshape(dec_batch, dec_seq, n_heads, head_dim))
            outs["fs"].append(sample_major(lf_s[:, :n_heads]))
            outs["cs"].append(jnp.concatenate(
                [state_conv[e][:, dec_seq:], sample_major(glu_s)], axis=1))
        else:
            o = layer // 2
            bs_full = jnp.repeat(b_spatial[o].T, d_c // n_groups, axis=1)
            ws_small = jnp.repeat(
                jnp.transpose(w_spatial[o][:, :dec_seq, :dec_seq], (1, 2, 0)),
                d_c // n_groups, axis=2).reshape(dec_seq * dec_seq, d_c)
            xp, xs, v_rows = _odd(
                xp, xs, g_mix, w_in_c[o].astype(BF), row(sgu_ln_g[o]), row(sgu_ln_b[o]),
                w_spatial[o].astype(BF), bs_full, ws_small, w_out_c[o].astype(BF),
                g_ffn, wu, wd, dec_batch=dec_batch, dec_seq=dec_seq)
            outs["chs"].append(jnp.swapaxes(v_rows.reshape(dec_seq, dec_batch, d_c), 0, 1))

    y_prompt = xp.reshape(batch, seq, d)
    y_sample = jnp.swapaxes(xs.reshape(dec_seq, dec_batch, d), 0, 1)
    return (y_prompt, y_sample,
            jnp.stack(outs["kp"]), jnp.stack(outs["vp"]), jnp.stack(outs["fp"]),
            jnp.stack(outs["ks"]), jnp.stack(outs["vs"]), jnp.stack(outs["fs"]),
            jnp.stack(outs["cp"]), jnp.stack(outs["cs"]), jnp.stack(outs["chs"]))
```

```python
import functools

import jax
import jax.numpy as jnp
from jax import lax
from jax.experimental import pallas as pl
from jax.experimental.pallas import tpu as pltpu

BF = jnp.bfloat16
F32 = jnp.float32

TM = 512
LANES = 128
SUBLANES = 8
N_HEADS = 8
HEAD_DIM = 64
PAIR = 2 * HEAD_DIM
CONV_WIDTH = 31
HIST = CONV_WIDTH - 1
HIST_PAD = 32
CHUNK = 128
RMS_EPS = 1e-6
LN_EPS = 1e-5
NEG = -1e30
VMEM_LIMIT = 56 * 1024 * 1024


def _dot(a, b):
    return jnp.dot(a, b, preferred_element_type=F32)


def _dot_nt(a, b):
    return lax.dot_general(a, b, (((1,), (1,)), ((), ())), preferred_element_type=F32)


def _rms(x, g):
    ms = jnp.mean(x * x, axis=-1, keepdims=True)
    return x * lax.rsqrt(ms + RMS_EPS) * g


def _ln(x, g, b):
    mu = jnp.mean(x, axis=-1, keepdims=True)
    xc = x - mu
    var = jnp.mean(xc * xc, axis=-1, keepdims=True)
    return xc * lax.rsqrt(var + LN_EPS) * g + b


def _log_sigmoid(z):
    return jnp.minimum(z, 0.0) - jnp.log1p(jnp.exp(-jnp.abs(z)))


def _split3(x):
    hi = x.astype(BF)
    r1 = x - hi.astype(F32)
    mid = r1.astype(BF)
    lo = (r1 - mid.astype(F32)).astype(BF)
    return hi, mid, lo


def _ffn(x, g_ref, wu_ref, wd_ref, chunk=1024):
    hn = _rms(x, g_ref[...]).astype(BF)
    acc = None
    for c in range(wu_ref.shape[1] // chunk):
        a = _dot(hn, wu_ref[:, c * chunk:(c + 1) * chunk])
        a = jnp.maximum(a, 0.0)
        part = _dot((a * a).astype(BF), wd_ref[c * chunk:(c + 1) * chunk, :])
        acc = part if acc is None else acc + part
    return x + acc


def _resident(shape):
    nd = len(shape)
    return pl.BlockSpec(shape, lambda *_: (0,) * nd, pipeline_mode=pl.Buffered(1))


def _even_in_kernel(xp_ref, xs_ref, g_ref, w_ref, bfg_ref, qg_ref, kg_ref, bd_ref,
                    cw_ref, cb_ref, lg_ref, lb_ref,
                    ap_ref, qpm_ref, kpm_ref, vpm_ref, kp_ref, vp_ref, lfp_ref, fp_ref,
                    histp_ref, qs_ref, ks_ref, vs_ref, glus_ref, lfs_ref,
                    gbuf_ref, fcarry_ref, *, n_ptiles, tiles_per_seq):
    i = pl.program_id(0)
    is_s = i >= n_ptiles
    is_p = jnp.logical_not(is_s)
    seq_first = jnp.logical_and(is_p, i % tiles_per_seq == 0)
    seq_last = jnp.logical_and(is_p, i % tiles_per_seq == tiles_per_seq - 1)
    d_a = cw_ref.shape[1]
    d_b = N_HEADS * HEAD_DIM

    x = jnp.where(is_s, xs_ref[...], xp_ref[...])
    h = _rms(x, g_ref[...]).astype(BF)

    a_in = _dot(h, w_ref[:, 0:2 * d_a])
    glu = a_in[:, :d_a] * jax.nn.sigmoid(a_in[:, d_a:])
    c0 = 2 * d_a
    q = _dot(h, w_ref[:, c0:c0 + d_b])
    k = _dot(h, w_ref[:, c0 + d_b:c0 + 2 * d_b])
    v = _dot(h, w_ref[:, c0 + 2 * d_b:c0 + 3 * d_b])
    fg = _dot(h, w_ref[:, c0 + 3 * d_b:c0 + 3 * d_b + LANES])

    bd = bd_ref[...]
    qn = q * lax.rsqrt(_dot((q * q).astype(BF), bd) + RMS_EPS) * qg_ref[...]
    kn = k * lax.rsqrt(_dot((k * k).astype(BF), bd) + RMS_EPS) * kg_ref[...]
    logf = _log_sigmoid(fg + bfg_ref[...])

    q_b = (qn * (HEAD_DIM ** -0.5)).astype(BF)
    k_b = kn.astype(BF)
    v_b = v.astype(BF)
    for hp in range(d_b // PAIR):
        sl = slice(hp * PAIR, (hp + 1) * PAIR)
        qpm_ref[hp] = q_b[:, sl]
        kpm_ref[hp] = k_b[:, sl]
        vpm_ref[hp] = v_b[:, sl]

    @pl.when(is_s)
    def _():
        qs_ref[...] = qn * (HEAD_DIM ** -0.5)
        ks_ref[...] = kn
        vs_ref[...] = v
        glus_ref[...] = glu
        lfs_ref[...] = logf

    @pl.when(seq_first)
    def _():
        gbuf_ref[0:HIST_PAD, :] = jnp.zeros((HIST_PAD, d_a), F32)
        fcarry_ref[...] = jnp.zeros_like(fcarry_ref)

    @pl.when(is_p)
    def _():
        kp_ref[...] = kn
        vp_ref[...] = v
        lfp_ref[...] = logf

        r = lax.broadcasted_iota(jnp.int32, (CHUNK, CHUNK), 0)
        c = lax.broadcasted_iota(jnp.int32, (CHUNK, CHUNK), 1)
        tril = (r >= c).astype(F32).astype(BF)
        hi, mid, lo = _split3(logf)
        carry = fcarry_ref[...]
        for cb in range(TM // CHUNK):
            sl = slice(cb * CHUNK, (cb + 1) * CHUNK)
            fc = _dot(tril, hi[sl]) + _dot(tril, mid[sl]) + _dot(tril, lo[sl]) + carry
            fp_ref[sl, :] = fc
            carry = fc[CHUNK - 1:CHUNK, :]
        fcarry_ref[...] = carry

        gbuf_ref[HIST_PAD:HIST_PAD + TM, :] = glu
        acc = jnp.zeros((TM, d_a), F32)
        for t in range(CONV_WIDTH):
            acc = acc + cw_ref[t:t + 1, :] * gbuf_ref[pl.ds(HIST_PAD - HIST + t, TM), :]
        y = _ln(acc + cb_ref[...], lg_ref[...], lb_ref[...])
        ap_ref[...] = (y * jax.nn.sigmoid(y)).astype(BF)
        gbuf_ref[0:HIST_PAD, :] = gbuf_ref[TM:TM + HIST_PAD, :]

    @pl.when(seq_last)
    def _():
        histp_ref[...] = gbuf_ref[pl.ds(HIST_PAD - HIST, HIST), :]


def _even_in(xp, xs, g, w, bfg, qg, kg, bd, cw, cb, lg, lb, *, batch, seq):
    n_p, d = xp.shape
    d_a = cw.shape[1]
    d_b = N_HEADS * HEAD_DIM
    n_ptiles = n_p // TM
    tiles_per_seq = seq // TM
    n_tot = n_p + TM
    last_p = n_ptiles - 1
    p_row = lambda i: (jnp.minimum(i, last_p), 0)
    s_row = lambda i: (0, 0)
    pm_spec = pl.BlockSpec((d_b // PAIR, TM, PAIR), lambda i: (0, i, 0))
    kernel = functools.partial(_even_in_kernel, n_ptiles=n_ptiles, tiles_per_seq=tiles_per_seq)
    out_shape = (
        jax.ShapeDtypeStruct((n_p, d_a), BF),
        jax.ShapeDtypeStruct((d_b // PAIR, n_tot, PAIR), BF),
        jax.ShapeDtypeStruct((d_b // PAIR, n_tot, PAIR), BF),
        jax.ShapeDtypeStruct((d_b // PAIR, n_tot, PAIR), BF),
        jax.ShapeDtypeStruct((n_p, d_b), F32),
        jax.ShapeDtypeStruct((n_p, d_b), F32),
        jax.ShapeDtypeStruct((n_p, LANES), F32),
        jax.ShapeDtypeStruct((n_p, LANES), F32),
        jax.ShapeDtypeStruct((batch, HIST, d_a), F32),
        jax.ShapeDtypeStruct((TM, d_b), F32),
        jax.ShapeDtypeStruct((TM, d_b), F32),
        jax.ShapeDtypeStruct((TM, d_b), F32),
        jax.ShapeDtypeStruct((TM, d_a), F32),
        jax.ShapeDtypeStruct((TM, LANES), F32),
    )
    out_specs = (
        pl.BlockSpec((TM, d_a), p_row),
        pm_spec, pm_spec, pm_spec,
        pl.BlockSpec((TM, d_b), p_row),
        pl.BlockSpec((TM, d_b), p_row),
        pl.BlockSpec((TM, LANES), p_row),
        pl.BlockSpec((TM, LANES), p_row),
        pl.BlockSpec((None, HIST, d_a),
                     lambda i: (jnp.minimum(i // tiles_per_seq, batch - 1), 0, 0)),
        pl.BlockSpec((TM, d_b), s_row),
        pl.BlockSpec((TM, d_b), s_row),
        pl.BlockSpec((TM, d_b), s_row),
        pl.BlockSpec((TM, d_a), s_row),
        pl.BlockSpec((TM, LANES), s_row),
    )
    in_specs = [
        pl.BlockSpec((TM, d), p_row),
        _resident(xs.shape),
        _resident(g.shape), _resident(w.shape), _resident(bfg.shape),
        _resident(qg.shape), _resident(kg.shape), _resident(bd.shape),
        _resident(cw.shape), _resident(cb.shape), _resident(lg.shape), _resident(lb.shape),
    ]
    return pl.pallas_call(
        kernel,
        grid=(n_ptiles + 1,),
        in_specs=in_specs,
        out_specs=out_specs,
        out_shape=out_shape,
        scratch_shapes=[pltpu.VMEM((HIST_PAD + TM, d_a), F32),
                        pltpu.VMEM((1, LANES), F32)],
        compiler_params=pltpu.CompilerParams(
            dimension_semantics=("arbitrary",), vmem_limit_bytes=VMEM_LIMIT),
        name="even_in",
    )(xp, xs, g, w, bfg, qg, kg, bd, cw, cb, lg, lb)


def _sample_conv_kernel(glu_ref, hist_ref, cw_ref, cb_ref, lg_ref, lb_ref, a_ref,
                        *, dec_batch, dec_seq):
    d_a = cw_ref.shape[1]

    def full(j):
        if j < HIST:
            return hist_ref[:, j * d_a:(j + 1) * d_a]
        return glu_ref[(j - HIST) * dec_batch:(j - HIST + 1) * dec_batch, :]

    for t in range(dec_seq):
        acc = jnp.zeros((dec_batch, d_a), F32)
        for kk in range(CONV_WIDTH):
            acc = acc + cw_ref[kk:kk + 1, :] * full(t + kk)
        y = _ln(acc + cb_ref[...], lg_ref[...], lb_ref[...])
        a_ref[t * dec_batch:(t + 1) * dec_batch, :] = (y * jax.nn.sigmoid(y)).astype(BF)


def _sample_conv(glu_s, hist2d, cw, cb, lg, lb, *, dec_batch, dec_seq):
    kernel = functools.partial(_sample_conv_kernel, dec_batch=dec_batch, dec_seq=dec_seq)
    return pl.pallas_call(
        kernel,
        out_shape=jax.ShapeDtypeStruct(glu_s.shape, BF),
        compiler_params=pltpu.CompilerParams(vmem_limit_bytes=VMEM_LIMIT),
        name="sample_conv",
    )(glu_s, hist2d, cw, cb, lg, lb)


def _flash_kernel(q_ref, k_ref, v_ref, f0_ref, f1_ref, o_ref, *, tq):
    qi = pl.program_id(2)
    q2 = q_ref[...]
    lane = lax.broadcasted_iota(jnp.int32, (1, PAIR), 1)
    first = lane < HEAD_DIM
    zero = jnp.zeros_like(q2)
    qh = (jnp.where(first, q2, zero), jnp.where(first, zero, q2))
    f_refs = (f0_ref, f1_ref)
    q0 = pl.multiple_of(qi * tq, tq)
    f_base = tuple(f[:, pl.ds(q0, LANES)][:, 0:1] for f in f_refs)

    def tile(start, carry, masked):
        m, l, acc = carry
        k2 = k_ref[pl.ds(start, tq), :]
        v2 = v_ref[pl.ds(start, tq), :]
        new_m, new_l, alphas, pvs = [], [], [], []
        for hh in range(2):
            s = _dot_nt(qh[hh], k2) - (f_refs[hh][:, pl.ds(start, tq)] - f_base[hh])
            if masked:
                r = lax.broadcasted_iota(jnp.int32, (tq, tq), 0)
                c = lax.broadcasted_iota(jnp.int32, (tq, tq), 1)
                s = jnp.where(r >= c, s, NEG)
            m_new = jnp.maximum(m[hh], jnp.max(s, axis=1, keepdims=True))
            alpha = jnp.exp(m[hh] - m_new)
            p = jnp.exp(s - m_new)
            new_l.append(alpha * l[hh] + jnp.sum(p, axis=1, keepdims=True))
            new_m.append(m_new)
            alphas.append(alpha)
            pvs.append(_dot(p.astype(BF), v2))
        acc = acc * jnp.where(first, alphas[0], alphas[1]) + jnp.where(first, pvs[0], pvs[1])
        return tuple(new_m), tuple(new_l), acc

    col = jnp.full((tq, 1), NEG, F32)
    zcol = jnp.zeros((tq, 1), F32)
    init = ((col, col), (zcol, zcol), jnp.zeros((tq, PAIR), F32))

    def body(ki, carry):
        return tile(pl.multiple_of(ki * tq, tq), carry, False)

    carry = lax.fori_loop(0, qi, body, init)
    _, l, acc = tile(q0, carry, True)
    o_ref[...] = (acc / jnp.where(first, l[0], l[1])).astype(BF)


def _flash(q_pm, k_pm, v_pm, f_t, *, batch, seq):
    n_pairs = q_pm.shape[0]
    tq = TM
    nq = seq // tq
    kernel = functools.partial(_flash_kernel, tq=tq)
    kv_spec = pl.BlockSpec((None, seq, PAIR), lambda b, hp, qi: (hp, b, 0))
    return pl.pallas_call(
        kernel,
        grid=(batch, n_pairs, nq),
        in_specs=[
            pl.BlockSpec((None, tq, PAIR), lambda b, hp, qi: (hp, b * nq + qi, 0)),
            kv_spec, kv_spec,
            pl.BlockSpec((None, 1, seq), lambda b, hp, qi: (2 * hp, 0, b)),
            pl.BlockSpec((None, 1, seq), lambda b, hp, qi: (2 * hp + 1, 0, b)),
        ],
        out_specs=pl.BlockSpec((tq, PAIR), lambda b, hp, qi: (b * nq + qi, hp)),
        out_shape=jax.ShapeDtypeStruct((batch * seq, n_pairs * PAIR), BF),
        compiler_params=pltpu.CompilerParams(
            dimension_semantics=("arbitrary", "arbitrary", "arbitrary"),
            vmem_limit_bytes=VMEM_LIMIT),
        name="flash_prompt",
    )(q_pm, k_pm, v_pm, f_t, f_t)


def _decode_kernel(pt_ref, q_ref, kn_ref, vn_ref, lfn_ref, *rest, n_pages, dec_seq, page):
    del pt_ref
    k_refs = rest[:n_pages]
    v_refs = rest[n_pages:2 * n_pages]
    lf_refs = rest[2 * n_pages:3 * n_pages]
    o_ref, s_ref = rest[3 * n_pages:]
    rows = dec_seq * N_HEADS
    cols = page * N_HEADS
    shifts = [N_HEADS << s for s in range((page - 1).bit_length())]

    qm = q_ref[...].reshape(rows, HEAD_DIM).astype(BF)
    kn = kn_ref[...].reshape(rows, HEAD_DIM).astype(BF)
    vn = vn_ref[...].reshape(rows, HEAD_DIM).astype(BF)

    lf = jnp.concatenate([r[...] for r in lf_refs], axis=0)
    lane = lax.broadcasted_iota(jnp.int32, lf.shape, 1)
    sfx = lf
    for d in shifts:
        sfx = sfx + jnp.where(lane < cols - d, pltpu.roll(sfx, cols - d, axis=1), 0.0)
    excl = sfx - lf
    tot = jnp.where(lane < N_HEADS, sfx, 0.0)
    for d in shifts:
        tot = tot + pltpu.roll(tot, d, axis=1)
    bias = [None] * n_pages
    run = jnp.zeros((1, cols), F32)
    for p in range(n_pages - 1, -1, -1):
        bias[p] = excl[p:p + 1, :] + run
        run = run + tot[p:p + 1, :]

    lfn = lfn_ref[...]
    lane_n = lax.broadcasted_iota(jnp.int32, lfn.shape, 1)
    cs = lfn
    d = N_HEADS
    while d < rows:
        cs = cs + jnp.where(lane_n >= d, pltpu.roll(cs, d, axis=1), 0.0)
        d *= 2
    r_n = lax.broadcasted_iota(jnp.int32, (rows, rows), 0)
    c_n = lax.broadcasted_iota(jnp.int32, (rows, rows), 1)
    ok_n = jnp.logical_and(r_n % N_HEADS == c_n % N_HEADS, c_n // N_HEADS <= r_n // N_HEADS)
    s_new = jnp.where(ok_n, _dot_nt(qm, kn) - cs[:, :rows], NEG)

    r_p = lax.broadcasted_iota(jnp.int32, (rows, cols), 0)
    c_p = lax.broadcasted_iota(jnp.int32, (rows, cols), 1)
    ok_p = r_p % N_HEADS == c_p % N_HEADS
    mx = jnp.full((rows, cols), NEG, F32)
    for p in range(n_pages):
        k_b = k_refs[p][...].reshape(cols, HEAD_DIM).astype(BF)
        s = jnp.where(ok_p, _dot_nt(qm, k_b) + bias[p], NEG)
        s_ref[p] = s
        mx = jnp.maximum(mx, s)
    m = jnp.maximum(jnp.max(mx, axis=1, keepdims=True), jnp.max(s_new, axis=1, keepdims=True))

    pr = jnp.exp(s_new - m)
    l = jnp.sum(pr, axis=1, keepdims=True)
    acc = _dot(pr.astype(BF), vn)
    psum = jnp.zeros((rows, cols), F32)
    for p in range(n_pages):
        v_b = v_refs[p][...].reshape(cols, HEAD_DIM).astype(BF)
        pr = jnp.exp(s_ref[p] - m)
        psum = psum + pr
        acc = acc + _dot(pr.astype(BF), v_b)
    l = l + jnp.sum(psum, axis=1, keepdims=True)
    o_ref[...] = (acc / l).reshape(dec_seq, N_HEADS, HEAD_DIM)


def _decode(page_table_flat, q_s, k_s, v_s, lfn_row, cache_k, cache_v, cache_lf_flat, *,
            layer, dec_batch, dec_seq):
    n_pages = page_table_flat.shape[0] // dec_batch
    page = cache_k.shape[2]
    tok_spec = pl.BlockSpec((dec_seq, None, N_HEADS, HEAD_DIM), lambda b, pt: (0, b, 0, 0))

    def page_map(p, tail, b, pt):
        return (layer, pt[b * n_pages + p]) + (0,) * tail

    kv_specs = [pl.BlockSpec((None, None, page, N_HEADS, HEAD_DIM),
                             functools.partial(page_map, p, 3)) for p in range(n_pages)]
    lf_specs = [pl.BlockSpec((None, None, 1, page * N_HEADS),
                             functools.partial(page_map, p, 2)) for p in range(n_pages)]
    kernel = functools.partial(_decode_kernel, n_pages=n_pages, dec_seq=dec_seq, page=page)
    grid_spec = pltpu.PrefetchScalarGridSpec(
        num_scalar_prefetch=1,
        grid=(dec_batch,),
        in_specs=[tok_spec, tok_spec, tok_spec,
                  pl.BlockSpec((None, 1, LANES), lambda b, pt: (b, 0, 0))]
        + kv_specs + kv_specs + lf_specs,
        out_specs=tok_spec,
        scratch_shapes=[pltpu.VMEM((n_pages, dec_seq * N_HEADS, page * N_HEADS), F32)],
    )
    return pl.pallas_call(
        kernel,
        grid_spec=grid_spec,
        out_shape=jax.ShapeDtypeStruct((dec_seq, dec_batch, N_HEADS, HEAD_DIM), F32),
        compiler_params=pltpu.CompilerParams(
            dimension_semantics=("arbitrary",), vmem_limit_bytes=VMEM_LIMIT),
        name="decode_attn",
    )(page_table_flat, q_s, k_s, v_s, lfn_row,
      *([cache_k] * n_pages), *([cache_v] * n_pages), *([cache_lf_flat] * n_pages))


def _even_out_kernel(xp_ref, xs_ref, ap_ref, as_ref, tp_ref, ts_ref, woa_ref, wob_ref,
                     g_ref, wu_ref, wd_ref, yp_ref, ys_ref, *, n_ptiles):
    i = pl.program_id(0)
    is_s = i >= n_ptiles
    x = jnp.where(is_s, xs_ref[...], xp_ref[...])
    a = jnp.where(is_s, as_ref[...], ap_ref[...])
    t = jnp.where(is_s, ts_ref[...], tp_ref[...])
    x = x + (_dot(a, woa_ref[...]) + _dot(t, wob_ref[...]))
    y = _ffn(x, g_ref, wu_ref, wd_ref)

    @pl.when(is_s)
    def _():
        ys_ref[...] = y

    @pl.when(jnp.logical_not(is_s))
    def _():
        yp_ref[...] = y


def _even_out(xp, xs, a_p, a_s, t_p, t_s, woa, wob, g, wu, wd):
    n_p, d = xp.shape
    n_ptiles = n_p // TM
    last_p = n_ptiles - 1
    p_row = lambda i: (jnp.minimum(i, last_p), 0)
    kernel = functools.partial(_even_out_kernel, n_ptiles=n_ptiles)
    return pl.pallas_call(
        kernel,
        grid=(n_ptiles + 1,),
        in_specs=[
            pl.BlockSpec((TM, d), p_row), _resident(xs.shape),
            pl.BlockSpec((TM, a_p.shape[1]), p_row), _resident(a_s.shape),
            pl.BlockSpec((TM, t_p.shape[1]), p_row), _resident(t_s.shape),
            _resident(woa.shape), _resident(wob.shape),
            _resident(g.shape), _resident(wu.shape), _resident(wd.shape),
        ],
        out_specs=(pl.BlockSpec((TM, d), p_row), pl.BlockSpec((TM, d), lambda i: (0, 0))),
        out_shape=(jax.ShapeDtypeStruct(xp.shape, F32), jax.ShapeDtypeStruct(xs.shape, F32)),
        compiler_params=pltpu.CompilerParams(
            dimension_semantics=("arbitrary",), vmem_limit_bytes=VMEM_LIMIT),
        name="even_out",
    )(xp, xs, a_p, a_s, t_p, t_s, woa, wob, g, wu, wd)


def _odd_kernel(xp_ref, xs_ref, gm_ref, wi_ref, lg_ref, lb_ref, ws_ref, bs_ref, wsm_ref,
                wo_ref, gf_ref, wu_ref, wd_ref, yp_ref, ys_ref, vs_ref, mix_ref,
                *, n_ptiles, dec_batch, dec_seq):
    i = pl.program_id(0)
    is_s = i >= n_ptiles
    d_c = wo_ref.shape[0]
    n_groups = ws_ref.shape[0]
    c_hd = d_c // n_groups

    x = jnp.where(is_s, xs_ref[...], xp_ref[...])
    h = _rms(x, gm_ref[...]).astype(BF)
    u = jax.nn.gelu(_dot(h, wi_ref[:, :d_c]))
    v = _ln(jax.nn.gelu(_dot(h, wi_ref[:, d_c:])), lg_ref[...], lb_ref[...])
    v_b = v.astype(BF)

    @pl.when(jnp.logical_not(is_s))
    def _():
        r = lax.broadcasted_iota(jnp.int32, (CHUNK, CHUNK), 0)
        c = lax.broadcasted_iota(jnp.int32, (CHUNK, CHUNK), 1)
        for hg in range(n_groups):
            w = jnp.where(r >= c, ws_ref[hg], jnp.zeros((CHUNK, CHUNK), BF))
            cols = slice(hg * c_hd, (hg + 1) * c_hd)
            for cb in range(TM // CHUNK):
                rows = slice(cb * CHUNK, (cb + 1) * CHUNK)
                mix_ref[rows, cols] = _dot(w, v_b[rows, cols]) + bs_ref[:, cols]

    @pl.when(is_s)
    def _():
        vs_ref[...] = v
        v_r = v_b.astype(F32)
        for t in range(dec_seq):
            acc = jnp.zeros((dec_batch, d_c), F32) + bs_ref[t:t + 1, :]
            for j in range(t + 1):
                wt = wsm_ref[t * dec_seq + j:t * dec_seq + j + 1, :].astype(BF).astype(F32)
                acc = acc + wt * v_r[j * dec_batch:(j + 1) * dec_batch, :]
            mix_ref[t * dec_batch:(t + 1) * dec_batch, :] = acc

    gte = (u * mix_ref[...]).astype(BF)
    x = x + _dot(gte, wo_ref[...])
    y = _ffn(x, gf_ref, wu_ref, wd_ref)

    @pl.when(is_s)
    def _():
        ys_ref[...] = y

    @pl.when(jnp.logical_not(is_s))
    def _():
        yp_ref[...] = y


def _odd(xp, xs, gm, wi, lg, lb, ws, bs_full, ws_small, wo, gf, wu, wd, *, dec_batch, dec_seq):
    n_p, d = xp.shape
    d_c = wo.shape[0]
    n_ptiles = n_p // TM
    last_p = n_ptiles - 1
    p_row = lambda i: (jnp.minimum(i, last_p), 0)
    s_row = lambda i: (0, 0)
    kernel = functools.partial(_odd_kernel, n_ptiles=n_ptiles, dec_batch=dec_batch,
                               dec_seq=dec_seq)
    return pl.pallas_call(
        kernel,
        grid=(n_ptiles + 1,),
        in_specs=[
            pl.BlockSpec((TM, d), p_row), _resident(xs.shape),
            _resident(gm.shape), _resident(wi.shape), _resident(lg.shape), _resident(lb.shape),
            _resident(ws.shape), _resident(bs_full.shape), _resident(ws_small.shape),
            _resident(wo.shape), _resident(gf.shape), _resident(wu.shape), _resident(wd.shape),
        ],
        out_specs=(pl.BlockSpec((TM, d), p_row), pl.BlockSpec((TM, d), s_row),
                   pl.BlockSpec((TM, d_c), s_row)),
        out_shape=(jax.ShapeDtypeStruct(xp.shape, F32), jax.ShapeDtypeStruct(xs.shape, F32),
                   jax.ShapeDtypeStruct((TM, d_c), F32)),
        scratch_shapes=[pltpu.VMEM((TM, d_c), F32)],
        compiler_params=pltpu.CompilerParams(
            dimension_semantics=("arbitrary",), vmem_limit_bytes=VMEM_LIMIT),
        name="odd_layer",
    )(xp, xs, gm, wi, lg, lb, ws, bs_full, ws_small, wo, gf, wu, wd)


def kernel(x_prompt, x_sample, cache_k, cache_v, cache_logf, state_conv, page_table,
           norm_mix_g, norm_ffn_g, w_in_ab, b_forget, q_norm_g, k_norm_g,
           conv_w, conv_b, conv_ln_g, conv_ln_b, w_out_ab,
           w_in_c, sgu_ln_g, sgu_ln_b, w_spatial, b_spatial, w_out_c,
           w_ff_up, w_ff_down):
    batch, seq, d = x_prompt.shape
    dec_batch, dec_seq, _ = x_sample.shape
    depth = norm_mix_g.shape[0]
    n_ab, n_pool, page, n_heads, head_dim = cache_k.shape
    d_b = n_heads * head_dim
    d_a = conv_w.shape[2]
    n_groups = w_spatial.shape[1]
    d_c = w_out_c.shape[1]
    assert dec_batch * dec_seq == TM and seq % TM == 0 and page == LANES
    assert n_heads == N_HEADS and head_dim == HEAD_DIM and conv_w.shape[1] == CONV_WIDTH
    assert w_spatial.shape[2] == CHUNK and dec_seq <= SUBLANES

    xp = x_prompt.reshape(batch * seq, d)
    xs = jnp.swapaxes(x_sample, 0, 1).reshape(TM, d)

    row = lambda a: a.reshape(1, -1).astype(F32)
    clf_flat = cache_logf.reshape(n_ab, n_pool, 1, page * n_heads)
    pt_flat = page_table.reshape(-1)
    hist2d = state_conv.reshape(n_ab, dec_batch, HIST * d_a)
    head_of = jnp.arange(d_b) // head_dim
    bd = (head_of[:, None] == head_of[None, :]).astype(BF) * (1.0 / head_dim)

    outs = {k: [] for k in ("kp", "vp", "fp", "cp", "ks", "vs", "fs", "cs", "chs")}
    for layer in range(depth):
        g_mix = row(norm_mix_g[layer])
        g_ffn = row(norm_ffn_g[layer])
        wu = w_ff_up[layer].astype(BF)
        wd = w_ff_down[layer].astype(BF)
        if layer % 2 == 0:
            e = layer // 2
            w_in = jnp.pad(w_in_ab[e], ((0, 0), (0, LANES - n_heads))).astype(BF)
            bfg = jnp.pad(b_forget[e], (0, LANES - n_heads)).reshape(1, LANES)
            qg = row(jnp.tile(q_norm_g[e], n_heads))
            kg = row(jnp.tile(k_norm_g[e], n_heads))
            cw, cb = conv_w[e], row(conv_b[e])
            lg, lb = row(conv_ln_g[e]), row(conv_ln_b[e])
            (a_p, q_pm, k_pm, v_pm, k_p, v_p, lf_p, f_p, hist_p,
             q_s, k_s, v_s, glu_s, lf_s) = _even_in(
                xp, xs, g_mix, w_in, bfg, qg, kg, bd, cw, cb, lg, lb, batch=batch, seq=seq)

            a_s = _sample_conv(glu_s, hist2d[e], cw, cb, lg, lb,
                               dec_batch=dec_batch, dec_seq=dec_seq)

            f_t = f_p[:, :n_heads].T.reshape(n_heads, 1, batch * seq)
            t_p = _flash(q_pm, k_pm, v_pm, f_t, batch=batch, seq=seq)

            lf_s4 = lf_s[:, :n_heads].reshape(dec_seq, dec_batch, n_heads)
            lfn_row = jnp.pad(
                jnp.swapaxes(lf_s4, 0, 1).reshape(dec_batch, 1, dec_seq * n_heads),
                ((0, 0), (0, 0), (0, LANES - dec_seq * n_heads)))
            tok = lambda a: a.reshape(dec_seq, dec_batch, n_heads, head_dim)
            t_s = _decode(pt_flat, tok(q_s), tok(k_s), tok(v_s), lfn_row,
                          cache_k, cache_v, clf_flat,
                          layer=e, dec_batch=dec_batch, dec_seq=dec_seq)
            t_s = t_s.reshape(TM, d_b).astype(BF)

            wo = w_out_ab[e].astype(BF)
            xp, xs = _even_out(xp, xs, a_p, a_s, t_p, t_s, wo[:d_a], wo[d_a:], g_ffn, wu, wd)

            sample_major = lambda a: jnp.swapaxes(a.reshape(dec_seq, dec_batch, -1), 0, 1)
            outs["kp"].append(k_p.reshape(batch, seq, n_heads, head_dim))
            outs["vp"].append(v_p.reshape(batch, seq, n_heads, head_dim))
            outs["fp"].append(lf_p[:, :n_heads].reshape(batch, seq, n_heads))
            outs["cp"].append(hist_p)
            outs["ks"].append(sample_major(k_s).reshape(dec_batch, dec_seq, n_heads, head_dim))
            outs["vs"].append(sample_major(v_s).reshape(dec_batch, dec_seq, n_heads, head_dim))
            outs["fs"].append(sample_major(lf_s[:, :n_heads]))
            outs["cs"].append(jnp.concatenate(
                [state_conv[e][:, dec_seq:], sample_major(glu_s)], axis=1))
        else:
            o = layer // 2
            bs_full = jnp.repeat(b_spatial[o].T, d_c // n_groups, axis=1)
            ws_small = jnp.repeat(
                jnp.transpose(w_spatial[o][:, :dec_seq, :dec_seq], (1, 2, 0)),
                d_c // n_groups, axis=2).reshape(dec_seq * dec_seq, d_c)
            xp, xs, v_rows = _odd(
                xp, xs, g_mix, w_in_c[o].astype(BF), row(sgu_ln_g[o]), row(sgu_ln_b[o]),
                w_spatial[o].astype(BF), bs_full, ws_small, w_out_c[o].astype(BF),
                g_ffn, wu, wd, dec_batch=dec_batch, dec_seq=dec_seq)
            outs["chs"].append(jnp.swapaxes(v_rows.reshape(dec_seq, dec_batch, d_c), 0, 1))

    y_prompt = xp.reshape(batch, seq, d)
    y_sample = jnp.swapaxes(xs.reshape(dec_seq, dec_batch, d), 0, 1)
    return (y_prompt, y_sample,
            jnp.stack(outs["kp"]), jnp.stack(outs["vp"]), jnp.stack(outs["fp"]),
            jnp.stack(outs["ks"]), jnp.stack(outs["vs"]), jnp.stack(outs["fs"]),
            jnp.stack(outs["cp"]), jnp.stack(outs["cs"]), jnp.stack(outs["chs"]))
```

```python
import functools

import jax
import jax.numpy as jnp
from jax import lax
from jax.experimental import pallas as pl
from jax.experimental.pallas import tpu as pltpu

BF = jnp.bfloat16
F32 = jnp.float32

TM = 512
LANES = 128
SUBLANES = 8
N_HEADS = 8
HEAD_DIM = 64
PAIR = 2 * HEAD_DIM
CONV_WIDTH = 31
HIST = CONV_WIDTH - 1
HIST_PAD = 32
CHUNK = 128
RMS_EPS = 1e-6
LN_EPS = 1e-5
NEG = -1e30
VMEM_LIMIT = 56 * 1024 * 1024


def _dot(a, b):
    return jnp.dot(a, b, preferred_element_type=F32)


def _dot_nt(a, b):
    return lax.dot_general(a, b, (((1,), (1,)), ((), ())), preferred_element_type=F32)


def _rms(x, g):
    ms = jnp.mean(x * x, axis=-1, keepdims=True)
    return x * lax.rsqrt(ms + RMS_EPS) * g


def _ln(x, g, b):
    mu = jnp.mean(x, axis=-1, keepdims=True)
    xc = x - mu
    var = jnp.mean(xc * xc, axis=-1, keepdims=True)
    return xc * lax.rsqrt(var + LN_EPS) * g + b


def _log_sigmoid(z):
    return jnp.minimum(z, 0.0) - jnp.log1p(jnp.exp(-jnp.abs(z)))


def _split3(x):
    hi = x.astype(BF)
    r1 = x - hi.astype(F32)
    mid = r1.astype(BF)
    lo = (r1 - mid.astype(F32)).astype(BF)
    return hi, mid, lo


def _ffn(x, g_ref, wu_ref, wd_ref, chunk=1024):
    hn = _rms(x, g_ref[...]).astype(BF)
    acc = None
    for c in range(wu_ref.shape[1] // chunk):
        a = _dot(hn, wu_ref[:, c * chunk:(c + 1) * chunk])
        a = jnp.maximum(a, 0.0)
        part = _dot((a * a).astype(BF), wd_ref[c * chunk:(c + 1) * chunk, :])
        acc = part if acc is None else acc + part
    return x + acc


def _resident(shape):
    nd = len(shape)
    return pl.BlockSpec(shape, lambda *_: (0,) * nd, pipeline_mode=pl.Buffered(1))


def _even_in_kernel(xp_ref, xs_ref, g_ref, w_ref, bfg_ref, qg_ref, kg_ref, bd_ref,
                    cw_ref, cb_ref, lg_ref, lb_ref,
                    ap_ref, qpm_ref, kpm_ref, vpm_ref, kp_ref, vp_ref, lfp_ref, fp_ref,
                    histp_ref, qs_ref, ks_ref, vs_ref, glus_ref, lfs_ref,
                    gbuf_ref, fcarry_ref, *, n_ptiles, tiles_per_seq):
    i = pl.program_id(0)
    is_s = i >= n_ptiles
    is_p = jnp.logical_not(is_s)
    seq_first = jnp.logical_and(is_p, i % tiles_per_seq == 0)
    seq_last = jnp.logical_and(is_p, i % tiles_per_seq == tiles_per_seq - 1)
    d_a = cw_ref.shape[1]
    d_b = N_HEADS * HEAD_DIM

    x = jnp.where(is_s, xs_ref[...], xp_ref[...])
    h = _rms(x, g_ref[...]).astype(BF)

    a_in = _dot(h, w_ref[:, 0:2 * d_a])
    glu = a_in[:, :d_a] * jax.nn.sigmoid(a_in[:, d_a:])
    c0 = 2 * d_a
    q = _dot(h, w_ref[:, c0:c0 + d_b])
    k = _dot(h, w_ref[:, c0 + d_b:c0 + 2 * d_b])
    v = _dot(h, w_ref[:, c0 + 2 * d_b:c0 + 3 * d_b])
    fg = _dot(h, w_ref[:, c0 + 3 * d_b:c0 + 3 * d_b + LANES])

    bd = bd_ref[...]
    qn = q * lax.rsqrt(_dot((q * q).astype(BF), bd) + RMS_EPS) * qg_ref[...]
    kn = k * lax.rsqrt(_dot((k * k).astype(BF), bd) + RMS_EPS) * kg_ref[...]
    logf = _log_sigmoid(fg + bfg_ref[...])

    q_b = (qn * (HEAD_DIM ** -0.5)).astype(BF)
    k_b = kn.astype(BF)
    v_b = v.astype(BF)
    for hp in range(d_b // PAIR):
        sl = slice(hp * PAIR, (hp + 1) * PAIR)
        qpm_ref[hp] = q_b[:, sl]
        kpm_ref[hp] = k_b[:, sl]
        vpm_ref[hp] = v_b[:, sl]

    @pl.when(is_s)
    def _():
        qs_ref[...] = qn * (HEAD_DIM ** -0.5)
        ks_ref[...] = kn
        vs_ref[...] = v
        glus_ref[...] = glu
        lfs_ref[...] = logf

    @pl.when(seq_first)
    def _():
        gbuf_ref[0:HIST_PAD, :] = jnp.zeros((HIST_PAD, d_a), F32)
        fcarry_ref[...] = jnp.zeros_like(fcarry_ref)

    @pl.when(is_p)
    def _():
        kp_ref[...] = kn
        vp_ref[...] = v
        lfp_ref[...] = logf

        r = lax.broadcasted_iota(jnp.int32, (CHUNK, CHUNK), 0)
        c = lax.broadcasted_iota(jnp.int32, (CHUNK, CHUNK), 1)
        tril = (r >= c).astype(F32).astype(BF)
        hi, mid, lo = _split3(logf)
        carry = fcarry_ref[...]
        for cb in range(TM // CHUNK):
            sl = slice(cb * CHUNK, (cb + 1) * CHUNK)
            fc = _dot(tril, hi[sl]) + _dot(tril, mid[sl]) + _dot(tril, lo[sl]) + carry
            fp_ref[sl, :] = fc
            carry = fc[CHUNK - 1:CHUNK, :]
        fcarry_ref[...] = carry

        gbuf_ref[HIST_PAD:HIST_PAD + TM, :] = glu
        acc = jnp.zeros((TM, d_a), F32)
        for t in range(CONV_WIDTH):
            acc = acc + cw_ref[t:t + 1, :] * gbuf_ref[pl.ds(HIST_PAD - HIST + t, TM), :]
        y = _ln(acc + cb_ref[...], lg_ref[...], lb_ref[...])
        ap_ref[...] = (y * jax.nn.sigmoid(y)).astype(BF)
        gbuf_ref[0:HIST_PAD, :] = gbuf_ref[TM:TM + HIST_PAD, :]

    @pl.when(seq_last)
    def _():
        histp_ref[...] = gbuf_ref[pl.ds(HIST_PAD - HIST, HIST), :]


def _even_in(xp, xs, g, w, bfg, qg, kg, bd, cw, cb, lg, lb, *, batch, seq):
    n_p, d = xp.shape
    d_a = cw.shape[1]
    d_b = N_HEADS * HEAD_DIM
    n_ptiles = n_p // TM
    tiles_per_seq = seq // TM
    n_tot = n_p + TM
    last_p = n_ptiles - 1
    p_row = lambda i: (jnp.minimum(i, last_p), 0)
    s_row = lambda i: (0, 0)
    pm_spec = pl.BlockSpec((d_b // PAIR, TM, PAIR), lambda i: (0, i, 0))
    kernel = functools.partial(_even_in_kernel, n_ptiles=n_ptiles, tiles_per_seq=tiles_per_seq)
    out_shape = (
        jax.ShapeDtypeStruct((n_p, d_a), BF),
        jax.ShapeDtypeStruct((d_b // PAIR, n_tot, PAIR), BF),
        jax.ShapeDtypeStruct((d_b // PAIR, n_tot, PAIR), BF),
        jax.ShapeDtypeStruct((d_b // PAIR, n_tot, PAIR), BF),
        jax.ShapeDtypeStruct((n_p, d_b), F32),
        jax.ShapeDtypeStruct((n_p, d_b), F32),
        jax.ShapeDtypeStruct((n_p, LANES), F32),
        jax.ShapeDtypeStruct((n_p, LANES), F32),
        jax.ShapeDtypeStruct((batch, HIST, d_a), F32),
        jax.ShapeDtypeStruct((TM, d_b), F32),
        jax.ShapeDtypeStruct((TM, d_b), F32),
        jax.ShapeDtypeStruct((TM, d_b), F32),
        jax.ShapeDtypeStruct((TM, d_a), F32),
        jax.ShapeDtypeStruct((TM, LANES), F32),
    )
    out_specs = (
        pl.BlockSpec((TM, d_a), p_row),
        pm_spec, pm_spec, pm_spec,
        pl.BlockSpec((TM, d_b), p_row),
        pl.BlockSpec((TM, d_b), p_row),
        pl.BlockSpec((TM, LANES), p_row),
        pl.BlockSpec((TM, LANES), p_row),
        pl.BlockSpec((None, HIST, d_a),
                     lambda i: (jnp.minimum(i // tiles_per_seq, batch - 1), 0, 0)),
        pl.BlockSpec((TM, d_b), s_row),
        pl.BlockSpec((TM, d_b), s_row),
        pl.BlockSpec((TM, d_b), s_row),
        pl.BlockSpec((TM, d_a), s_row),
        pl.BlockSpec((TM, LANES), s_row),
    )
    in_specs = [
        pl.BlockSpec((TM, d), p_row),
        _resident(xs.shape),
        _resident(g.shape), _resident(w.shape), _resident(bfg.shape),
        _resident(qg.shape), _resident(kg.shape), _resident(bd.shape),
        _resident(cw.shape), _resident(cb.shape), _resident(lg.shape), _resident(lb.shape),
    ]
    return pl.pallas_call(
        kernel,
        grid=(n_ptiles + 1,),
        in_specs=in_specs,
        out_specs=out_specs,
        out_shape=out_shape,
        scratch_shapes=[pltpu.VMEM((HIST_PAD + TM, d_a), F32),
                        pltpu.VMEM((1, LANES), F32)],
        compiler_params=pltpu.CompilerParams(
            dimension_semantics=("arbitrary",), vmem_limit_bytes=VMEM_LIMIT),
        name="even_in",
    )(xp, xs, g, w, bfg, qg, kg, bd, cw, cb, lg, lb)


def _sample_conv_kernel(glu_ref, hist_ref, cw_ref, cb_ref, lg_ref, lb_ref, a_ref,
                        *, dec_batch, dec_seq):
    d_a = cw_ref.shape[1]

    def full(j):
        if j < HIST:
            return hist_ref[j]
        return glu_ref[(j - HIST) * dec_batch:(j - HIST + 1) * dec_batch, :]

    for t in range(dec_seq):
        acc = jnp.zeros((dec_batch, d_a), F32)
        for kk in range(CONV_WIDTH):
            acc = acc + cw_ref[kk:kk + 1, :] * full(t + kk)
        y = _ln(acc + cb_ref[...], lg_ref[...], lb_ref[...])
        a_ref[t * dec_batch:(t + 1) * dec_batch, :] = (y * jax.nn.sigmoid(y)).astype(BF)


def _sample_conv(glu_s, hist2d, cw, cb, lg, lb, *, dec_batch, dec_seq):
    kernel = functools.partial(_sample_conv_kernel, dec_batch=dec_batch, dec_seq=dec_seq)
    return pl.pallas_call(
        kernel,
        out_shape=jax.ShapeDtypeStruct(glu_s.shape, BF),
        compiler_params=pltpu.CompilerParams(vmem_limit_bytes=VMEM_LIMIT),
        name="sample_conv",
    )(glu_s, hist2d, cw, cb, lg, lb)


def _flash_kernel(q_ref, k_ref, v_ref, f0_ref, f1_ref, o_ref, *, tq):
    qi = pl.program_id(2)
    q2 = q_ref[...]
    lane = lax.broadcasted_iota(jnp.int32, (1, PAIR), 1)
    first = lane < HEAD_DIM
    zero = jnp.zeros_like(q2)
    qh = (jnp.where(first, q2, zero), jnp.where(first, zero, q2))
    f_refs = (f0_ref, f1_ref)
    q0 = pl.multiple_of(qi * tq, tq)
    f_base = tuple(f[:, pl.ds(q0, LANES)][:, 0:1] for f in f_refs)

    def tile(start, carry, masked):
        m, l, acc = carry
        k2 = k_ref[pl.ds(start, tq), :]
        v2 = v_ref[pl.ds(start, tq), :]
        new_m, new_l, alphas, pvs = [], [], [], []
        for hh in range(2):
            s = _dot_nt(qh[hh], k2) - (f_refs[hh][:, pl.ds(start, tq)] - f_base[hh])
            if masked:
                r = lax.broadcasted_iota(jnp.int32, (tq, tq), 0)
                c = lax.broadcasted_iota(jnp.int32, (tq, tq), 1)
                s = jnp.where(r >= c, s, NEG)
            m_new = jnp.maximum(m[hh], jnp.max(s, axis=1, keepdims=True))
            alpha = jnp.exp(m[hh] - m_new)
            p = jnp.exp(s - m_new)
            new_l.append(alpha * l[hh] + jnp.sum(p, axis=1, keepdims=True))
            new_m.append(m_new)
            alphas.append(alpha)
            pvs.append(_dot(p.astype(BF), v2))
        acc = acc * jnp.where(first, alphas[0], alphas[1]) + jnp.where(first, pvs[0], pvs[1])
        return tuple(new_m), tuple(new_l), acc

    col = jnp.full((tq, 1), NEG, F32)
    zcol = jnp.zeros((tq, 1), F32)
    init = ((col, col), (zcol, zcol), jnp.zeros((tq, PAIR), F32))

    def body(ki, carry):
        return tile(pl.multiple_of(ki * tq, tq), carry, False)

    carry = lax.fori_loop(0, qi, body, init)
    _, l, acc = tile(q0, carry, True)
    o_ref[...] = (acc / jnp.where(first, l[0], l[1])).astype(BF)


def _flash(q_pm, k_pm, v_pm, f_t, *, batch, seq):
    n_pairs = q_pm.shape[0]
    tq = TM
    nq = seq // tq
    kernel = functools.partial(_flash_kernel, tq=tq)
    kv_spec = pl.BlockSpec((None, seq, PAIR), lambda b, hp, qi: (hp, b, 0))
    return pl.pallas_call(
        kernel,
        grid=(batch, n_pairs, nq),
        in_specs=[
            pl.BlockSpec((None, tq, PAIR), lambda b, hp, qi: (hp, b * nq + qi, 0)),
            kv_spec, kv_spec,
            pl.BlockSpec((None, 1, seq), lambda b, hp, qi: (2 * hp, 0, b)),
            pl.BlockSpec((None, 1, seq), lambda b, hp, qi: (2 * hp + 1, 0, b)),
        ],
        out_specs=pl.BlockSpec((tq, PAIR), lambda b, hp, qi: (b * nq + qi, hp)),
        out_shape=jax.ShapeDtypeStruct((batch * seq, n_pairs * PAIR), BF),
        compiler_params=pltpu.CompilerParams(
            dimension_semantics=("arbitrary", "arbitrary", "arbitrary"),
            vmem_limit_bytes=VMEM_LIMIT),
        name="flash_prompt",
    )(q_pm, k_pm, v_pm, f_t, f_t)


def _decode_kernel(pt_ref, q_ref, kn_ref, vn_ref, lfn_ref, *rest, n_pages, dec_seq, page):
    del pt_ref
    k_refs = rest[:n_pages]
    v_refs = rest[n_pages:2 * n_pages]
    lf_refs = rest[2 * n_pages:3 * n_pages]
    o_ref = rest[3 * n_pages]
    kx_ref, vx_ref = rest[3 * n_pages + 1:]
    d_b = N_HEADS * HEAD_DIM
    rows = dec_seq * N_HEADS

    @pl.when(pl.program_id(0) == 0)
    def _():
        kx_ref[...] = jnp.zeros_like(kx_ref)
        vx_ref[...] = jnp.zeros_like(vx_ref)

    kx_ref[0:dec_seq, :] = kn_ref[:, 0, :]
    vx_ref[0:dec_seq, :] = vn_ref[:, 0, :]

    lane_h = lax.broadcasted_iota(jnp.int32, (N_HEADS, d_b), 1) // HEAD_DIM
    row_h = lax.broadcasted_iota(jnp.int32, (N_HEADS, d_b), 0)
    head_mask = lane_h == row_h
    q4 = q_ref[:, 0, :].astype(F32)
    q32 = jnp.where(head_mask[None], q4[:, None, :], 0.0).reshape(rows, d_b).astype(BF)

    lf = jnp.concatenate([r[...] for r in lf_refs], axis=0)
    lane = lax.broadcasted_iota(jnp.int32, lf.shape, 1)
    sfx = lf
    d = 1
    while d < page:
        sfx = sfx + jnp.where(lane < page - d, pltpu.roll(sfx, page - d, axis=1), 0.0)
        d *= 2
    excl = sfx - lf
    tot = sfx[:, 0:1]
    bias = [None] * n_pages
    run = jnp.zeros((N_HEADS, 1), F32)
    for p in range(n_pages - 1, -1, -1):
        sl = slice(p * N_HEADS, (p + 1) * N_HEADS)
        bias[p] = excl[sl] + run
        run = run + tot[sl]

    s_pages = []
    for p in range(n_pages):
        k_t = k_refs[p][...].reshape(d_b, page).astype(BF)
        s = _dot(q32, k_t).reshape(dec_seq, N_HEADS, page)
        s_pages.append(s + bias[p][None])

    lfn = lfn_ref[...]
    lane8 = lax.broadcasted_iota(jnp.int32, lfn.shape, 1)
    cs = lfn
    d = 1
    while d < dec_seq:
        cs = cs + jnp.where(lane8 >= d, pltpu.roll(cs, d, axis=1), 0.0)
        d *= 2
    s_new = _dot_nt(q32, kx_ref[...].astype(BF)).reshape(dec_seq, N_HEADS, page) - cs[None]
    step = lax.broadcasted_iota(jnp.int32, (dec_seq, N_HEADS, page), 0)
    key = lax.broadcasted_iota(jnp.int32, (dec_seq, N_HEADS, page), 2)
    s_new = jnp.where(key <= step, s_new, NEG)

    mx = s_new
    for s in s_pages:
        mx = jnp.maximum(mx, s)
    m = jnp.max(mx, axis=2, keepdims=True)
    p_new = jnp.exp(s_new - m)
    psum = p_new
    o = _dot(p_new.reshape(rows, page).astype(BF), vx_ref[...].astype(BF))
    for p in range(n_pages):
        pp = jnp.exp(s_pages[p] - m)
        psum = psum + pp
        v_t = v_refs[p][...].reshape(d_b, page).astype(BF)
        o = o + _dot_nt(pp.reshape(rows, page).astype(BF), v_t)
    l = jnp.sum(psum, axis=2, keepdims=True)
    o = o.reshape(dec_seq, N_HEADS, d_b) / l
    o_ref[:, 0, :] = jnp.sum(jnp.where(head_mask[None], o, 0.0), axis=1).astype(BF)


def _decode(page_table_flat, q_s, k_s, v_s, lfn_t, cache_kt, cache_vt, cache_lf_t, *,
            layer, dec_batch, dec_seq):
    n_pages = page_table_flat.shape[0] // dec_batch
    page = cache_kt.shape[4]
    d_b = N_HEADS * HEAD_DIM
    tok_spec = pl.BlockSpec((dec_seq, None, 1, d_b), lambda b, pt: (0, b, 0, 0))

    def page_map(p, tail, b, pt):
        return (layer, pt[b * n_pages + p]) + (0,) * tail

    kv_specs = [pl.BlockSpec((None, None, N_HEADS, HEAD_DIM, page),
                             functools.partial(page_map, p, 3)) for p in range(n_pages)]
    lf_specs = [pl.BlockSpec((None, None, N_HEADS, page),
                             functools.partial(page_map, p, 2)) for p in range(n_pages)]
    kernel = functools.partial(_decode_kernel, n_pages=n_pages, dec_seq=dec_seq, page=page)
    grid_spec = pltpu.PrefetchScalarGridSpec(
        num_scalar_prefetch=1,
        grid=(dec_batch,),
        in_specs=[tok_spec, tok_spec, tok_spec,
                  pl.BlockSpec((None, N_HEADS, page), lambda b, pt: (b, 0, 0))]
        + kv_specs + kv_specs + lf_specs,
        out_specs=tok_spec,
        scratch_shapes=[pltpu.VMEM((page, d_b), F32), pltpu.VMEM((page, d_b), F32)],
    )
    return pl.pallas_call(
        kernel,
        grid_spec=grid_spec,
        out_shape=jax.ShapeDtypeStruct((dec_seq, dec_batch, 1, d_b), BF),
        compiler_params=pltpu.CompilerParams(
            dimension_semantics=("arbitrary",), vmem_limit_bytes=VMEM_LIMIT),
        name="decode_attn",
    )(page_table_flat, q_s, k_s, v_s, lfn_t,
      *([cache_kt] * n_pages), *([cache_vt] * n_pages), *([cache_lf_t] * n_pages))


def _even_out_kernel(xp_ref, xs_ref, ap_ref, as_ref, tp_ref, ts_ref, woa_ref, wob_ref,
                     g_ref, wu_ref, wd_ref, yp_ref, ys_ref, *, n_ptiles):
    i = pl.program_id(0)
    is_s = i >= n_ptiles
    x = jnp.where(is_s, xs_ref[...], xp_ref[...])
    a = jnp.where(is_s, as_ref[...], ap_ref[...])
    t = jnp.where(is_s, ts_ref[...], tp_ref[...])
    x = x + (_dot(a, woa_ref[...]) + _dot(t, wob_ref[...]))
    y = _ffn(x, g_ref, wu_ref, wd_ref)

    @pl.when(is_s)
    def _():
        ys_ref[...] = y

    @pl.when(jnp.logical_not(is_s))
    def _():
        yp_ref[...] = y


def _even_out(xp, xs, a_p, a_s, t_p, t_s, woa, wob, g, wu, wd):
    n_p, d = xp.shape
    n_ptiles = n_p // TM
    last_p = n_ptiles - 1
    p_row = lambda i: (jnp.minimum(i, last_p), 0)
    kernel = functools.partial(_even_out_kernel, n_ptiles=n_ptiles)
    return pl.pallas_call(
        kernel,
        grid=(n_ptiles + 1,),
        in_specs=[
            pl.BlockSpec((TM, d), p_row), _resident(xs.shape),
            pl.BlockSpec((TM, a_p.shape[1]), p_row), _resident(a_s.shape),
            pl.BlockSpec((TM, t_p.shape[1]), p_row), _resident(t_s.shape),
            _resident(woa.shape), _resident(wob.shape),
            _resident(g.shape), _resident(wu.shape), _resident(wd.shape),
        ],
        out_specs=(pl.BlockSpec((TM, d), p_row), pl.BlockSpec((TM, d), lambda i: (0, 0))),
        out_shape=(jax.ShapeDtypeStruct(xp.shape, F32), jax.ShapeDtypeStruct(xs.shape, F32)),
        compiler_params=pltpu.CompilerParams(
            dimension_semantics=("arbitrary",), vmem_limit_bytes=VMEM_LIMIT),
        name="even_out",
    )(xp, xs, a_p, a_s, t_p, t_s, woa, wob, g, wu, wd)


def _odd_kernel(xp_ref, xs_ref, gm_ref, wi_ref, lg_ref, lb_ref, ws_ref, bs_ref, wsm_ref,
                wo_ref, gf_ref, wu_ref, wd_ref, yp_ref, ys_ref, vs_ref, mix_ref,
                *, n_ptiles, dec_batch, dec_seq):
    i = pl.program_id(0)
    is_s = i >= n_ptiles
    d_c = wo_ref.shape[0]
    n_groups = ws_ref.shape[0]
    c_hd = d_c // n_groups

    x = jnp.where(is_s, xs_ref[...], xp_ref[...])
    h = _rms(x, gm_ref[...]).astype(BF)
    u = jax.nn.gelu(_dot(h, wi_ref[:, :d_c]))
    v = _ln(jax.nn.gelu(_dot(h, wi_ref[:, d_c:])), lg_ref[...], lb_ref[...])
    v_b = v.astype(BF)

    @pl.when(jnp.logical_not(is_s))
    def _():
        r = lax.broadcasted_iota(jnp.int32, (CHUNK, CHUNK), 0)
        c = lax.broadcasted_iota(jnp.int32, (CHUNK, CHUNK), 1)
        for hg in range(n_groups):
            w = jnp.where(r >= c, ws_ref[hg], jnp.zeros((CHUNK, CHUNK), BF))
            cols = slice(hg * c_hd, (hg + 1) * c_hd)
            for cb in range(TM // CHUNK):
                rows = slice(cb * CHUNK, (cb + 1) * CHUNK)
                mix_ref[rows, cols] = _dot(w, v_b[rows, cols]) + bs_ref[:, cols]

    @pl.when(is_s)
    def _():
        vs_ref[...] = v
        v_r = v_b.astype(F32)
        for t in range(dec_seq):
            acc = jnp.zeros((dec_batch, d_c), F32) + bs_ref[t:t + 1, :]
            for j in range(t + 1):
                wt = wsm_ref[t * dec_seq + j:t * dec_seq + j + 1, :].astype(BF).astype(F32)
                acc = acc + wt * v_r[j * dec_batch:(j + 1) * dec_batch, :]
            mix_ref[t * dec_batch:(t + 1) * dec_batch, :] = acc

    gte = (u * mix_ref[...]).astype(BF)
    x = x + _dot(gte, wo_ref[...])
    y = _ffn(x, gf_ref, wu_ref, wd_ref)

    @pl.when(is_s)
    def _():
        ys_ref[...] = y

    @pl.when(jnp.logical_not(is_s))
    def _():
        yp_ref[...] = y


def _odd(xp, xs, gm, wi, lg, lb, ws, bs_full, ws_small, wo, gf, wu, wd, *, dec_batch, dec_seq):
    n_p, d = xp.shape
    d_c = wo.shape[0]
    n_ptiles = n_p // TM
    last_p = n_ptiles - 1
    p_row = lambda i: (jnp.minimum(i, last_p), 0)
    s_row = lambda i: (0, 0)
    kernel = functools.partial(_odd_kernel, n_ptiles=n_ptiles, dec_batch=dec_batch,
                               dec_seq=dec_seq)
    return pl.pallas_call(
        kernel,
        grid=(n_ptiles + 1,),
        in_specs=[
            pl.BlockSpec((TM, d), p_row), _resident(xs.shape),
            _resident(gm.shape), _resident(wi.shape), _resident(lg.shape), _resident(lb.shape),
            _resident(ws.shape), _resident(bs_full.shape), _resident(ws_small.shape),
            _resident(wo.shape), _resident(gf.shape), _resident(wu.shape), _resident(wd.shape),
        ],
        out_specs=(pl.BlockSpec((TM, d), p_row), pl.BlockSpec((TM, d), s_row),
                   pl.BlockSpec((TM, d_c), s_row)),
        out_shape=(jax.ShapeDtypeStruct(xp.shape, F32), jax.ShapeDtypeStruct(xs.shape, F32),
                   jax.ShapeDtypeStruct((TM, d_c), F32)),
        scratch_shapes=[pltpu.VMEM((TM, d_c), F32)],
        compiler_params=pltpu.CompilerParams(
            dimension_semantics=("arbitrary",), vmem_limit_bytes=VMEM_LIMIT),
        name="odd_layer",
    )(xp, xs, gm, wi, lg, lb, ws, bs_full, ws_small, wo, gf, wu, wd)


def kernel(x_prompt, x_sample, cache_k, cache_v, cache_logf, state_conv, page_table,
           norm_mix_g, norm_ffn_g, w_in_ab, b_forget, q_norm_g, k_norm_g,
           conv_w, conv_b, conv_ln_g, conv_ln_b, w_out_ab,
           w_in_c, sgu_ln_g, sgu_ln_b, w_spatial, b_spatial, w_out_c,
           w_ff_up, w_ff_down):
    batch, seq, d = x_prompt.shape
    dec_batch, dec_seq, _ = x_sample.shape
    depth = norm_mix_g.shape[0]
    n_ab, n_pool, page, n_heads, head_dim = cache_k.shape
    d_b = n_heads * head_dim
    d_a = conv_w.shape[2]
    n_groups = w_spatial.shape[1]
    d_c = w_out_c.shape[1]
    assert dec_batch * dec_seq == TM and seq % TM == 0 and page == LANES
    assert n_heads == N_HEADS and head_dim == HEAD_DIM and conv_w.shape[1] == CONV_WIDTH
    assert w_spatial.shape[2] == CHUNK and dec_seq <= SUBLANES

    xp = x_prompt.reshape(batch * seq, d)
    xs = jnp.swapaxes(x_sample, 0, 1).reshape(TM, d)

    row = lambda a: a.reshape(1, -1).astype(F32)
    ck_t = jnp.transpose(cache_k, (0, 1, 3, 4, 2))
    cv_t = jnp.transpose(cache_v, (0, 1, 3, 4, 2))
    clf_t = jnp.swapaxes(cache_logf, 2, 3)
    hist_t = jnp.swapaxes(state_conv, 1, 2)
    pt_flat = page_table.reshape(-1)
    head_of = jnp.arange(d_b) // head_dim
    bd = (head_of[:, None] == head_of[None, :]).astype(BF) * (1.0 / head_dim)

    outs = {k: [] for k in ("kp", "vp", "fp", "cp", "ks", "vs", "fs", "cs", "chs")}
    for layer in range(depth):
        g_mix = row(norm_mix_g[layer])
        g_ffn = row(norm_ffn_g[layer])
        wu = w_ff_up[layer].astype(BF)
        wd = w_ff_down[layer].astype(BF)
        if layer % 2 == 0:
            e = layer // 2
            w_in = jnp.pad(w_in_ab[e], ((0, 0), (0, LANES - n_heads))).astype(BF)
            bfg = jnp.pad(b_forget[e], (0, LANES - n_heads)).reshape(1, LANES)
            qg = row(jnp.tile(q_norm_g[e], n_heads))
            kg = row(jnp.tile(k_norm_g[e], n_heads))
            cw, cb = conv_w[e], row(conv_b[e])
            lg, lb = row(conv_ln_g[e]), row(conv_ln_b[e])
            (a_p, q_pm, k_pm, v_pm, k_p, v_p, lf_p, f_p, hist_p,
             q_s, k_s, v_s, glu_s, lf_s) = _even_in(
                xp, xs, g_mix, w_in, bfg, qg, kg, bd, cw, cb, lg, lb, batch=batch, seq=seq)

            a_s = _sample_conv(glu_s, hist_t[e], cw, cb, lg, lb,
                               dec_batch=dec_batch, dec_seq=dec_seq)

            f_t = f_p[:, :n_heads].T.reshape(n_heads, 1, batch * seq)
            t_p = _flash(q_pm, k_pm, v_pm, f_t, batch=batch, seq=seq)

            lf_s4 = lf_s[:, :n_heads].reshape(dec_seq, dec_batch, n_heads)
            lfn_t = jnp.pad(jnp.transpose(lf_s4, (1, 2, 0)),
                            ((0, 0), (0, 0), (0, page - dec_seq)))
            tok = lambda a: a.reshape(dec_seq, dec_batch, 1, d_b)
            t_s = _decode(pt_flat, tok(q_s), tok(k_s), tok(v_s), lfn_t, ck_t, cv_t, clf_t,
                          layer=e, dec_batch=dec_batch, dec_seq=dec_seq).reshape(TM, d_b)

            wo = w_out_ab[e].astype(BF)
            xp, xs = _even_out(xp, xs, a_p, a_s, t_p, t_s, wo[:d_a], wo[d_a:], g_ffn, wu, wd)

            sample_major = lambda a: jnp.swapaxes(a.reshape(dec_seq, dec_batch, -1), 0, 1)
            outs["kp"].append(k_p.reshape(batch, seq, n_heads, head_dim))
            outs["vp"].append(v_p.reshape(batch, seq, n_heads, head_dim))
            outs["fp"].append(lf_p[:, :n_heads].reshape(batch, seq, n_heads))
            outs["cp"].append(hist_p)
            outs["ks"].append(sample_major(k_s).reshape(dec_batch, dec_seq, n_heads, head_dim))
            outs["vs"].append(sample_major(v_s).reshape(dec_batch, dec_seq, n_heads, head_dim))
            outs["fs"].append(sample_major(lf_s[:, :n_heads]))
            outs["cs"].append(jnp.swapaxes(jnp.concatenate(
                [hist_t[e][dec_seq:], glu_s.reshape(dec_seq, dec_batch, d_a)], axis=0), 0, 1))
        else:
            o = layer // 2
            bs_full = jnp.repeat(b_spatial[o].T, d_c // n_groups, axis=1)
            ws_small = jnp.repeat(
                jnp.transpose(w_spatial[o][:, :dec_seq, :dec_seq], (1, 2, 0)),
                d_c // n_groups, axis=2).reshape(dec_seq * dec_seq, d_c)
            xp, xs, v_rows = _odd(
                xp, xs, g_mix, w_in_c[o].astype(BF), row(sgu_ln_g[o]), row(sgu_ln_b[o]),
                w_spatial[o].astype(BF), bs_full, ws_small, w_out_c[o].astype(BF),
                g_ffn, wu, wd, dec_batch=dec_batch, dec_seq=dec_seq)
            outs["chs"].append(jnp.swapaxes(v_rows.reshape(dec_seq, dec_batch, d_c), 0, 1))

    y_prompt = xp.reshape(batch, seq, d)
    y_sample = jnp.swapaxes(xs.reshape(dec_seq, dec_batch, d), 0, 1)
    return (y_prompt, y_sample,
            jnp.stack(outs["kp"]), jnp.stack(outs["vp"]), jnp.stack(outs["fp"]),
            jnp.stack(outs["ks"]), jnp.stack(outs["vs"]), jnp.stack(outs["fs"]),
            jnp.stack(outs["cp"]), jnp.stack(outs["cs"]), jnp.stack(outs["chs"]))
```

```python
import functools

import jax
import jax.numpy as jnp
from jax import lax
from jax.experimental import pallas as pl
from jax.experimental.pallas import tpu as pltpu

BF = jnp.bfloat16
F32 = jnp.float32

TM = 512
LANES = 128
SUBLANES = 8
N_HEADS = 8
HEAD_DIM = 64
PAIR = 2 * HEAD_DIM
CONV_WIDTH = 31
HIST = CONV_WIDTH - 1
HIST_PAD = 32
CHUNK = 128
RMS_EPS = 1e-6
LN_EPS = 1e-5
NEG = -1e30
VMEM_LIMIT = 56 * 1024 * 1024


def _dot(a, b):
    return jnp.dot(a, b, preferred_element_type=F32)


def _dot_nt(a, b):
    return lax.dot_general(a, b, (((1,), (1,)), ((), ())), preferred_element_type=F32)


def _rms(x, g):
    ms = jnp.mean(x * x, axis=-1, keepdims=True)
    return x * lax.rsqrt(ms + RMS_EPS) * g


def _ln(x, g, b):
    mu = jnp.mean(x, axis=-1, keepdims=True)
    xc = x - mu
    var = jnp.mean(xc * xc, axis=-1, keepdims=True)
    return xc * lax.rsqrt(var + LN_EPS) * g + b


def _log_sigmoid(z):
    return jnp.minimum(z, 0.0) - jnp.log1p(jnp.exp(-jnp.abs(z)))


def _split3(x):
    hi = x.astype(BF)
    r1 = x - hi.astype(F32)
    mid = r1.astype(BF)
    lo = (r1 - mid.astype(F32)).astype(BF)
    return hi, mid, lo


def _ffn(x, g_ref, wu_ref, wd_ref, chunk=1024):
    hn = _rms(x, g_ref[...]).astype(BF)
    acc = None
    for c in range(wu_ref.shape[1] // chunk):
        a = _dot(hn, wu_ref[:, c * chunk:(c + 1) * chunk])
        a = jnp.maximum(a, 0.0)
        part = _dot((a * a).astype(BF), wd_ref[c * chunk:(c + 1) * chunk, :])
        acc = part if acc is None else acc + part
    return x + acc


def _resident(shape):
    nd = len(shape)
    return pl.BlockSpec(shape, lambda *_: (0,) * nd, pipeline_mode=pl.Buffered(1))


def _even_in_kernel(xp_ref, xs_ref, g_ref, w_ref, bfg_ref, qg_ref, kg_ref, bd_ref,
                    cw_ref, cb_ref, lg_ref, lb_ref, sel_ref,
                    ap_ref, qt_ref, kaug_ref, vt_ref, ktp_ref, vtp_ref, lft_ref,
                    histp_ref, qs_ref, ks_ref, vs_ref, glus_ref, lfs_ref,
                    gbuf_ref, fcarry_ref, *, n_ptiles, tiles_per_seq):
    i = pl.program_id(0)
    is_s = i >= n_ptiles
    is_p = jnp.logical_not(is_s)
    seq_first = jnp.logical_and(is_p, i % tiles_per_seq == 0)
    seq_last = jnp.logical_and(is_p, i % tiles_per_seq == tiles_per_seq - 1)
    d_a = cw_ref.shape[1]
    d_b = N_HEADS * HEAD_DIM

    x = jnp.where(is_s, xs_ref[...], xp_ref[...])
    h = _rms(x, g_ref[...]).astype(BF)

    a_in = _dot(h, w_ref[:, 0:2 * d_a])
    glu = a_in[:, :d_a] * jax.nn.sigmoid(a_in[:, d_a:])
    c0 = 2 * d_a
    q = _dot(h, w_ref[:, c0:c0 + d_b])
    k = _dot(h, w_ref[:, c0 + d_b:c0 + 2 * d_b])
    v = _dot(h, w_ref[:, c0 + 2 * d_b:c0 + 3 * d_b])
    fg = _dot(h, w_ref[:, c0 + 3 * d_b:c0 + 3 * d_b + LANES])

    bd = bd_ref[...]
    qn = q * lax.rsqrt(_dot((q * q).astype(BF), bd) + RMS_EPS) * qg_ref[...]
    kn = k * lax.rsqrt(_dot((k * k).astype(BF), bd) + RMS_EPS) * kg_ref[...]
    logf = _log_sigmoid(fg + bfg_ref[...])

    q_sc = qn * (HEAD_DIM ** -0.5)

    @pl.when(is_s)
    def _():
        qs_ref[...] = q_sc
        ks_ref[...] = kn
        vs_ref[...] = v
        glus_ref[...] = glu
        lfs_ref[...] = logf

    @pl.when(seq_first)
    def _():
        gbuf_ref[0:HIST_PAD, :] = jnp.zeros((HIST_PAD, d_a), F32)
        fcarry_ref[...] = jnp.zeros_like(fcarry_ref)

    @pl.when(is_p)
    def _():
        q_t = q_sc.T
        k_t = kn.T
        v_t = v.T
        ktp_ref[...] = k_t
        vtp_ref[...] = v_t
        lft_ref[...] = logf.T[0:N_HEADS, :]
        for hp in range(d_b // PAIR):
            sl = slice(hp * PAIR, (hp + 1) * PAIR)
            qt_ref[hp] = q_t[sl, :].astype(BF)
            vt_ref[hp] = v_t[sl, :].astype(BF)

        r = lax.broadcasted_iota(jnp.int32, (CHUNK, CHUNK), 0)
        c = lax.broadcasted_iota(jnp.int32, (CHUNK, CHUNK), 1)
        tril = (r >= c).astype(F32).astype(BF)
        hi, mid, lo = _split3(logf)
        carry = fcarry_ref[...]
        f_chunks = []
        for cb in range(TM // CHUNK):
            sl = slice(cb * CHUNK, (cb + 1) * CHUNK)
            fc = _dot(tril, hi[sl]) + _dot(tril, mid[sl]) + _dot(tril, lo[sl]) + carry
            f_chunks.append(fc)
            carry = fc[CHUNK - 1:CHUNK, :]
        fcarry_ref[...] = carry

        nf_hi, nf_mid, nf_lo = _split3(-jnp.concatenate(f_chunks, axis=0))
        lane = lax.broadcasted_iota(jnp.int32, (1, LANES), 1)
        parts = jnp.where(
            lane < N_HEADS, nf_hi.astype(F32),
            jnp.where(lane < 2 * N_HEADS, pltpu.roll(nf_mid.astype(F32), N_HEADS, axis=1),
                      pltpu.roll(nf_lo.astype(F32), 2 * N_HEADS, axis=1))).astype(BF)
        aug = _dot(parts, sel_ref[...])
        for hd in range(N_HEADS):
            pair = kn[:, (hd // 2) * PAIR:(hd // 2 + 1) * PAIR]
            if hd % 2:
                pair = pltpu.roll(pair, HEAD_DIM, axis=1)
            kaug_ref[hd] = jnp.where(lane < HEAD_DIM, pair,
                                     aug[:, hd * LANES:(hd + 1) * LANES]).astype(BF)

        gbuf_ref[HIST_PAD:HIST_PAD + TM, :] = glu
        acc = jnp.zeros((TM, d_a), F32)
        for t in range(CONV_WIDTH):
            acc = acc + cw_ref[t:t + 1, :] * gbuf_ref[pl.ds(HIST_PAD - HIST + t, TM), :]
        y = _ln(acc + cb_ref[...], lg_ref[...], lb_ref[...])
        ap_ref[...] = (y * jax.nn.sigmoid(y)).astype(BF)
        gbuf_ref[0:HIST_PAD, :] = gbuf_ref[TM:TM + HIST_PAD, :]

    @pl.when(seq_last)
    def _():
        histp_ref[...] = gbuf_ref[pl.ds(HIST_PAD - HIST, HIST), :]


def _even_in(xp, xs, g, w, bfg, qg, kg, bd, cw, cb, lg, lb, sel, *, batch, seq):
    n_p, d = xp.shape
    d_a = cw.shape[1]
    d_b = N_HEADS * HEAD_DIM
    n_pairs = d_b // PAIR
    n_ptiles = n_p // TM
    tiles_per_seq = seq // TM
    last_p = n_ptiles - 1
    p_row = lambda i: (jnp.minimum(i, last_p), 0)
    p_col3 = lambda i: (0, 0, jnp.minimum(i, last_p))
    s_row = lambda i: (0, 0)
    seq_col = lambda i: (jnp.minimum(i, last_p) // tiles_per_seq, 0,
                         jnp.minimum(i, last_p) % tiles_per_seq)
    kernel = functools.partial(_even_in_kernel, n_ptiles=n_ptiles, tiles_per_seq=tiles_per_seq)
    out_shape = (
        jax.ShapeDtypeStruct((n_p, d_a), BF),
        jax.ShapeDtypeStruct((n_pairs, PAIR, n_p), BF),
        jax.ShapeDtypeStruct((N_HEADS, n_p, LANES), BF),
        jax.ShapeDtypeStruct((n_pairs, PAIR, n_p), BF),
        jax.ShapeDtypeStruct((batch, d_b, seq), F32),
        jax.ShapeDtypeStruct((batch, d_b, seq), F32),
        jax.ShapeDtypeStruct((batch, N_HEADS, seq), F32),
        jax.ShapeDtypeStruct((batch, HIST, d_a), F32),
        jax.ShapeDtypeStruct((TM, d_b), F32),
        jax.ShapeDtypeStruct((TM, d_b), F32),
        jax.ShapeDtypeStruct((TM, d_b), F32),
        jax.ShapeDtypeStruct((TM, d_a), F32),
        jax.ShapeDtypeStruct((TM, LANES), F32),
    )
    out_specs = (
        pl.BlockSpec((TM, d_a), p_row),
        pl.BlockSpec((n_pairs, PAIR, TM), p_col3),
        pl.BlockSpec((N_HEADS, TM, LANES), lambda i: (0, jnp.minimum(i, last_p), 0)),
        pl.BlockSpec((n_pairs, PAIR, TM), p_col3),
        pl.BlockSpec((None, d_b, TM), seq_col),
        pl.BlockSpec((None, d_b, TM), seq_col),
        pl.BlockSpec((None, N_HEADS, TM), seq_col),
        pl.BlockSpec((None, HIST, d_a),
                     lambda i: (jnp.minimum(i // tiles_per_seq, batch - 1), 0, 0)),
        pl.BlockSpec((TM, d_b), s_row),
        pl.BlockSpec((TM, d_b), s_row),
        pl.BlockSpec((TM, d_b), s_row),
        pl.BlockSpec((TM, d_a), s_row),
        pl.BlockSpec((TM, LANES), s_row),
    )
    in_specs = [
        pl.BlockSpec((TM, d), p_row),
        _resident(xs.shape),
        _resident(g.shape), _resident(w.shape), _resident(bfg.shape),
        _resident(qg.shape), _resident(kg.shape), _resident(bd.shape),
        _resident(cw.shape), _resident(cb.shape), _resident(lg.shape), _resident(lb.shape),
        _resident(sel.shape),
    ]
    return pl.pallas_call(
        kernel,
        grid=(n_ptiles + 1,),
        in_specs=in_specs,
        out_specs=out_specs,
        out_shape=out_shape,
        scratch_shapes=[pltpu.VMEM((HIST_PAD + TM, d_a), F32),
                        pltpu.VMEM((1, LANES), F32)],
        compiler_params=pltpu.CompilerParams(
            dimension_semantics=("arbitrary",), vmem_limit_bytes=VMEM_LIMIT),
        name="even_in",
    )(xp, xs, g, w, bfg, qg, kg, bd, cw, cb, lg, lb, sel)


def _sample_conv_kernel(glu_ref, hist_ref, cw_ref, cb_ref, lg_ref, lb_ref, a_ref,
                        *, dec_batch, dec_seq):
    d_a = cw_ref.shape[1]

    def full(j):
        if j < HIST:
            return hist_ref[j]
        return glu_ref[(j - HIST) * dec_batch:(j - HIST + 1) * dec_batch, :]

    for t in range(dec_seq):
        acc = jnp.zeros((dec_batch, d_a), F32)
        for kk in range(CONV_WIDTH):
            acc = acc + cw_ref[kk:kk + 1, :] * full(t + kk)
        y = _ln(acc + cb_ref[...], lg_ref[...], lb_ref[...])
        a_ref[t * dec_batch:(t + 1) * dec_batch, :] = (y * jax.nn.sigmoid(y)).astype(BF)


def _sample_conv(glu_s, hist2d, cw, cb, lg, lb, *, dec_batch, dec_seq):
    kernel = functools.partial(_sample_conv_kernel, dec_batch=dec_batch, dec_seq=dec_seq)
    return pl.pallas_call(
        kernel,
        out_shape=jax.ShapeDtypeStruct(glu_s.shape, BF),
        compiler_params=pltpu.CompilerParams(vmem_limit_bytes=VMEM_LIMIT),
        name="sample_conv",
    )(glu_s, hist2d, cw, cb, lg, lb)


def _flash_kernel(qt_ref, ka0_ref, ka1_ref, vt_ref, o_ref, s_ref, *, tq):
    qi = pl.program_id(2)
    q0 = pl.multiple_of(qi * tq, tq)
    q2 = qt_ref[...].astype(F32)
    row = lax.broadcasted_iota(jnp.int32, (PAIR, 1), 0)
    ones_rows = jnp.where(jnp.logical_and(row >= HEAD_DIM, row < HEAD_DIM + 3), 1.0, 0.0)
    swapped = jnp.concatenate([q2[HEAD_DIM:], q2[:HEAD_DIM]], axis=0)
    qa = tuple(jnp.where(row < HEAD_DIM, qq, ones_rows).astype(BF) for qq in (q2, swapped))
    ka = (ka0_ref, ka1_ref)

    def scores(ki, slot):
        start = pl.multiple_of(ki * tq, tq)
        for hh in range(2):
            s_ref[slot, hh] = _dot(ka[hh][pl.ds(start, tq), :], qa[hh])

    def tile(ki, slot, carry, masked):
        start = pl.multiple_of(ki * tq, tq)
        m, l, acc = carry
        new_m, new_l, new_acc = [], [], []
        for hh in range(2):
            s_t = s_ref[slot, hh]
            if masked:
                r = lax.broadcasted_iota(jnp.int32, (tq, tq), 0)
                c = lax.broadcasted_iota(jnp.int32, (tq, tq), 1)
                s_t = jnp.where(r <= c, s_t, NEG)
            m_new = jnp.maximum(m[hh], jnp.max(s_t, axis=0, keepdims=True))
            alpha = jnp.exp(m[hh] - m_new)
            p = jnp.exp(s_t - m_new)
            new_l.append(alpha * l[hh] + jnp.sum(p, axis=0, keepdims=True))
            new_m.append(m_new)
            v_t = vt_ref[hh * HEAD_DIM:(hh + 1) * HEAD_DIM, pl.ds(start, tq)]
            new_acc.append(alpha * acc[hh] + _dot(v_t, p.astype(BF)))
        return tuple(new_m), tuple(new_l), tuple(new_acc)

    neg = jnp.full((1, tq), NEG, F32)
    zrow = jnp.zeros((1, tq), F32)
    zacc = jnp.zeros((HEAD_DIM, tq), F32)
    init = ((neg, neg), (zrow, zrow), (zacc, zacc))

    def finish(carry):
        _, l, acc = carry
        out_t = jnp.concatenate([acc[0] / l[0], acc[1] / l[1]], axis=0)
        o_ref[...] = out_t.T.astype(BF)

    def pair(j, carry):
        scores(2 * j + 1, 1)
        carry = tile(2 * j, 0, carry, False)
        scores(2 * j + 2, 0)
        return tile(2 * j + 1, 1, carry, False)

    scores(0, 0)
    carry = lax.fori_loop(0, qi // 2, pair, init)

    @pl.when(qi % 2 == 0)
    def _():
        finish(tile(qi, 0, carry, True))

    @pl.when(qi % 2 == 1)
    def _():
        scores(qi, 1)
        finish(tile(qi, 1, tile(qi - 1, 0, carry, False), True))


def _flash(q_t, k_aug, v_t, *, batch, seq):
    n_pairs = q_t.shape[0]
    tq = TM
    nq = seq // tq
    kernel = functools.partial(_flash_kernel, tq=tq)
    t_spec = pl.BlockSpec((None, PAIR, seq), lambda b, hp, qi: (hp, 0, b))
    return pl.pallas_call(
        kernel,
        grid=(batch, n_pairs, nq),
        in_specs=[
            pl.BlockSpec((None, PAIR, tq), lambda b, hp, qi: (hp, 0, b * nq + qi)),
            pl.BlockSpec((None, seq, LANES), lambda b, hp, qi: (2 * hp, b, 0)),
            pl.BlockSpec((None, seq, LANES), lambda b, hp, qi: (2 * hp + 1, b, 0)),
            t_spec,
        ],
        out_specs=pl.BlockSpec((tq, PAIR), lambda b, hp, qi: (b * nq + qi, hp)),
        out_shape=jax.ShapeDtypeStruct((batch * seq, n_pairs * PAIR), BF),
        scratch_shapes=[pltpu.VMEM((2, 2, tq, tq), F32)],
        compiler_params=pltpu.CompilerParams(
            dimension_semantics=("arbitrary", "arbitrary", "arbitrary"),
            vmem_limit_bytes=VMEM_LIMIT),
        name="flash_prompt",
    )(q_t, k_aug, k_aug, v_t)


def _decode_kernel(pt_ref, q_ref, kn_ref, vn_ref, lfn_ref, *rest, n_pages, dec_seq, page):
    del pt_ref
    k_refs = rest[:n_pages]
    v_refs = rest[n_pages:2 * n_pages]
    lf_refs = rest[2 * n_pages:3 * n_pages]
    o_ref = rest[3 * n_pages]
    kx_ref, vx_ref = rest[3 * n_pages + 1:]
    d_b = N_HEADS * HEAD_DIM
    rows = dec_seq * N_HEADS

    @pl.when(pl.program_id(0) == 0)
    def _():
        kx_ref[...] = jnp.zeros_like(kx_ref)
        vx_ref[...] = jnp.zeros_like(vx_ref)

    kx_ref[0:dec_seq, :] = kn_ref[:, 0, :]
    vx_ref[0:dec_seq, :] = vn_ref[:, 0, :]

    lane_h = lax.broadcasted_iota(jnp.int32, (N_HEADS, d_b), 1) // HEAD_DIM
    row_h = lax.broadcasted_iota(jnp.int32, (N_HEADS, d_b), 0)
    head_mask = lane_h == row_h
    q4 = q_ref[:, 0, :].astype(F32)
    q32 = jnp.where(head_mask[None], q4[:, None, :], 0.0).reshape(rows, d_b).astype(BF)

    lf = jnp.concatenate([r[...] for r in lf_refs], axis=0)
    lane = lax.broadcasted_iota(jnp.int32, lf.shape, 1)
    sfx = lf
    d = 1
    while d < page:
        sfx = sfx + jnp.where(lane < page - d, pltpu.roll(sfx, page - d, axis=1), 0.0)
        d *= 2
    excl = sfx - lf
    tot = sfx[:, 0:1]
    bias = [None] * n_pages
    run = jnp.zeros((N_HEADS, 1), F32)
    for p in range(n_pages - 1, -1, -1):
        sl = slice(p * N_HEADS, (p + 1) * N_HEADS)
        bias[p] = excl[sl] + run
        run = run + tot[sl]

    s_pages = []
    for p in range(n_pages):
        k_t = k_refs[p][...].reshape(d_b, page).astype(BF)
        s = _dot(q32, k_t).reshape(dec_seq, N_HEADS, page)
        s_pages.append(s + bias[p][None])

    lfn = lfn_ref[...]
    lane8 = lax.broadcasted_iota(jnp.int32, lfn.shape, 1)
    cs = lfn
    d = 1
    while d < dec_seq:
        cs = cs + jnp.where(lane8 >= d, pltpu.roll(cs, d, axis=1), 0.0)
        d *= 2
    s_new = _dot_nt(q32, kx_ref[...].astype(BF)).reshape(dec_seq, N_HEADS, page) - cs[None]
    step = lax.broadcasted_iota(jnp.int32, (dec_seq, N_HEADS, page), 0)
    key = lax.broadcasted_iota(jnp.int32, (dec_seq, N_HEADS, page), 2)
    s_new = jnp.where(key <= step, s_new, NEG)

    mx = s_new
    for s in s_pages:
        mx = jnp.maximum(mx, s)
    m = jnp.max(mx, axis=2, keepdims=True)
    p_new = jnp.exp(s_new - m)
    psum = p_new
    o = _dot(p_new.reshape(rows, page).astype(BF), vx_ref[...].astype(BF))
    for p in range(n_pages):
        pp = jnp.exp(s_pages[p] - m)
        psum = psum + pp
        v_t = v_refs[p][...].reshape(d_b, page).astype(BF)
        o = o + _dot_nt(pp.reshape(rows, page).astype(BF), v_t)
    l = jnp.sum(psum, axis=2, keepdims=True)
    o = o.reshape(dec_seq, N_HEADS, d_b) / l
    o_ref[:, 0, :] = jnp.sum(jnp.where(head_mask[None], o, 0.0), axis=1).astype(BF)


def _decode(page_table_flat, q_s, k_s, v_s, lfn_t, cache_kt, cache_vt, cache_lf_t, *,
            layer, dec_batch, dec_seq):
    n_pages = page_table_flat.shape[0] // dec_batch
    page = cache_kt.shape[4]
    d_b = N_HEADS * HEAD_DIM
    tok_spec = pl.BlockSpec((dec_seq, None, 1, d_b), lambda b, pt: (0, b, 0, 0))

    def page_map(p, tail, b, pt):
        return (layer, pt[b * n_pages + p]) + (0,) * tail

    kv_specs = [pl.BlockSpec((None, None, N_HEADS, HEAD_DIM, page),
                             functools.partial(page_map, p, 3)) for p in range(n_pages)]
    lf_specs = [pl.BlockSpec((None, None, N_HEADS, page),
                             functools.partial(page_map, p, 2)) for p in range(n_pages)]
    kernel = functools.partial(_decode_kernel, n_pages=n_pages, dec_seq=dec_seq, page=page)
    grid_spec = pltpu.PrefetchScalarGridSpec(
        num_scalar_prefetch=1,
        grid=(dec_batch,),
        in_specs=[tok_spec, tok_spec, tok_spec,
                  pl.BlockSpec((None, N_HEADS, page), lambda b, pt: (b, 0, 0))]
        + kv_specs + kv_specs + lf_specs,
        out_specs=tok_spec,
        scratch_shapes=[pltpu.VMEM((page, d_b), F32), pltpu.VMEM((page, d_b), F32)],
    )
    return pl.pallas_call(
        kernel,
        grid_spec=grid_spec,
        out_shape=jax.ShapeDtypeStruct((dec_seq, dec_batch, 1, d_b), BF),
        compiler_params=pltpu.CompilerParams(
            dimension_semantics=("arbitrary",), vmem_limit_bytes=VMEM_LIMIT),
        name="decode_attn",
    )(page_table_flat, q_s, k_s, v_s, lfn_t,
      *([cache_kt] * n_pages), *([cache_vt] * n_pages), *([cache_lf_t] * n_pages))


def _even_out_kernel(xp_ref, xs_ref, ap_ref, as_ref, tp_ref, ts_ref, woa_ref, wob_ref,
                     g_ref, wu_ref, wd_ref, yp_ref, ys_ref, *, n_ptiles):
    i = pl.program_id(0)
    is_s = i >= n_ptiles
    x = jnp.where(is_s, xs_ref[...], xp_ref[...])
    a = jnp.where(is_s, as_ref[...], ap_ref[...])
    t = jnp.where(is_s, ts_ref[...], tp_ref[...])
    x = x + (_dot(a, woa_ref[...]) + _dot(t, wob_ref[...]))
    y = _ffn(x, g_ref, wu_ref, wd_ref)

    @pl.when(is_s)
    def _():
        ys_ref[...] = y

    @pl.when(jnp.logical_not(is_s))
    def _():
        yp_ref[...] = y


def _even_out(xp, xs, a_p, a_s, t_p, t_s, woa, wob, g, wu, wd):
    n_p, d = xp.shape
    n_ptiles = n_p // TM
    last_p = n_ptiles - 1
    p_row = lambda i: (jnp.minimum(i, last_p), 0)
    kernel = functools.partial(_even_out_kernel, n_ptiles=n_ptiles)
    return pl.pallas_call(
        kernel,
        grid=(n_ptiles + 1,),
        in_specs=[
            pl.BlockSpec((TM, d), p_row), _resident(xs.shape),
            pl.BlockSpec((TM, a_p.shape[1]), p_row), _resident(a_s.shape),
            pl.BlockSpec((TM, t_p.shape[1]), p_row), _resident(t_s.shape),
            _resident(woa.shape), _resident(wob.shape),
            _resident(g.shape), _resident(wu.shape), _resident(wd.shape),
        ],
        out_specs=(pl.BlockSpec((TM, d), p_row), pl.BlockSpec((TM, d), lambda i: (0, 0))),
        out_shape=(jax.ShapeDtypeStruct(xp.shape, F32), jax.ShapeDtypeStruct(xs.shape, F32)),
        compiler_params=pltpu.CompilerParams(
            dimension_semantics=("arbitrary",), vmem_limit_bytes=VMEM_LIMIT),
        name="even_out",
    )(xp, xs, a_p, a_s, t_p, t_s, woa, wob, g, wu, wd)


def _odd_kernel(xp_ref, xs_ref, gm_ref, wi_ref, lg_ref, lb_ref, ws_ref, bs_ref, wsm_ref,
                wo_ref, gf_ref, wu_ref, wd_ref, yp_ref, ys_ref, vs_ref, mix_ref,
                *, n_ptiles, dec_batch, dec_seq):
    i = pl.program_id(0)
    is_s = i >= n_ptiles
    d_c = wo_ref.shape[0]
    n_groups = ws_ref.shape[0]
    c_hd = d_c // n_groups

    x = jnp.where(is_s, xs_ref[...], xp_ref[...])
    h = _rms(x, gm_ref[...]).astype(BF)
    u = jax.nn.gelu(_dot(h, wi_ref[:, :d_c]))
    v = _ln(jax.nn.gelu(_dot(h, wi_ref[:, d_c:])), lg_ref[...], lb_ref[...])
    v_b = v.astype(BF)

    @pl.when(jnp.logical_not(is_s))
    def _():
        r = lax.broadcasted_iota(jnp.int32, (CHUNK, CHUNK), 0)
        c = lax.broadcasted_iota(jnp.int32, (CHUNK, CHUNK), 1)
        for hg in range(n_groups):
            w = jnp.where(r >= c, ws_ref[hg], jnp.zeros((CHUNK, CHUNK), BF))
            cols = slice(hg * c_hd, (hg + 1) * c_hd)
            for cb in range(TM // CHUNK):
                rows = slice(cb * CHUNK, (cb + 1) * CHUNK)
                mix_ref[rows, cols] = _dot(w, v_b[rows, cols]) + bs_ref[:, cols]

    @pl.when(is_s)
    def _():
        vs_ref[...] = v
        v_r = v_b.astype(F32)
        for t in range(dec_seq):
            acc = jnp.zeros((dec_batch, d_c), F32) + bs_ref[t:t + 1, :]
            for j in range(t + 1):
                wt = wsm_ref[t * dec_seq + j:t * dec_seq + j + 1, :].astype(BF).astype(F32)
                acc = acc + wt * v_r[j * dec_batch:(j + 1) * dec_batch, :]
            mix_ref[t * dec_batch:(t + 1) * dec_batch, :] = acc

    gte = (u * mix_ref[...]).astype(BF)
    x = x + _dot(gte, wo_ref[...])
    y = _ffn(x, gf_ref, wu_ref, wd_ref)

    @pl.when(is_s)
    def _():
        ys_ref[...] = y

    @pl.when(jnp.logical_not(is_s))
    def _():
        yp_ref[...] = y


def _odd(xp, xs, gm, wi, lg, lb, ws, bs_full, ws_small, wo, gf, wu, wd, *, dec_batch, dec_seq):
    n_p, d = xp.shape
    d_c = wo.shape[0]
    n_ptiles = n_p // TM
    last_p = n_ptiles - 1
    p_row = lambda i: (jnp.minimum(i, last_p), 0)
    s_row = lambda i: (0, 0)
    kernel = functools.partial(_odd_kernel, n_ptiles=n_ptiles, dec_batch=dec_batch,
                               dec_seq=dec_seq)
    return pl.pallas_call(
        kernel,
        grid=(n_ptiles + 1,),
        in_specs=[
            pl.BlockSpec((TM, d), p_row), _resident(xs.shape),
            _resident(gm.shape), _resident(wi.shape), _resident(lg.shape), _resident(lb.shape),
            _resident(ws.shape), _resident(bs_full.shape), _resident(ws_small.shape),
            _resident(wo.shape), _resident(gf.shape), _resident(wu.shape), _resident(wd.shape),
        ],
        out_specs=(pl.BlockSpec((TM, d), p_row), pl.BlockSpec((TM, d), s_row),
                   pl.BlockSpec((TM, d_c), s_row)),
        out_shape=(jax.ShapeDtypeStruct(xp.shape, F32), jax.ShapeDtypeStruct(xs.shape, F32),
                   jax.ShapeDtypeStruct((TM, d_c), F32)),
        scratch_shapes=[pltpu.VMEM((TM, d_c), F32)],
        compiler_params=pltpu.CompilerParams(
            dimension_semantics=("arbitrary",), vmem_limit_bytes=VMEM_LIMIT),
        name="odd_layer",
    )(xp, xs, gm, wi, lg, lb, ws, bs_full, ws_small, wo, gf, wu, wd)


def kernel(x_prompt, x_sample, cache_k, cache_v, cache_logf, state_conv, page_table,
           norm_mix_g, norm_ffn_g, w_in_ab, b_forget, q_norm_g, k_norm_g,
           conv_w, conv_b, conv_ln_g, conv_ln_b, w_out_ab,
           w_in_c, sgu_ln_g, sgu_ln_b, w_spatial, b_spatial, w_out_c,
           w_ff_up, w_ff_down):
    batch, seq, d = x_prompt.shape
    dec_batch, dec_seq, _ = x_sample.shape
    depth = norm_mix_g.shape[0]
    n_ab, n_pool, page, n_heads, head_dim = cache_k.shape
    d_b = n_heads * head_dim
    d_a = conv_w.shape[2]
    n_groups = w_spatial.shape[1]
    d_c = w_out_c.shape[1]
    assert dec_batch * dec_seq == TM and seq % TM == 0 and page == LANES
    assert n_heads == N_HEADS and head_dim == HEAD_DIM and conv_w.shape[1] == CONV_WIDTH
    assert w_spatial.shape[2] == CHUNK and dec_seq <= SUBLANES

    xp = x_prompt.reshape(batch * seq, d)
    xs = jnp.swapaxes(x_sample, 0, 1).reshape(TM, d)

    row = lambda a: a.reshape(1, -1).astype(F32)
    ck_t = jnp.transpose(cache_k, (0, 1, 3, 4, 2))
    cv_t = jnp.transpose(cache_v, (0, 1, 3, 4, 2))
    clf_t = jnp.swapaxes(cache_logf, 2, 3)
    hist_t = jnp.swapaxes(state_conv, 1, 2)
    pt_flat = page_table.reshape(-1)
    head_of = jnp.arange(d_b) // head_dim
    bd = (head_of[:, None] == head_of[None, :]).astype(BF) * (1.0 / head_dim)
    src = jnp.arange(LANES)
    dst = (src % n_heads) * LANES + head_dim + src // n_heads
    sel = jnp.logical_and(src[:, None] < 3 * n_heads,
                          jnp.arange(n_heads * LANES)[None, :] == dst[:, None]).astype(BF)

    outs = {k: [] for k in ("kp", "vp", "fp", "cp", "ks", "vs", "fs", "cs", "chs")}
    for layer in range(depth):
        g_mix = row(norm_mix_g[layer])
        g_ffn = row(norm_ffn_g[layer])
        wu = w_ff_up[layer].astype(BF)
        wd = w_ff_down[layer].astype(BF)
        if layer % 2 == 0:
            e = layer // 2
            w_in = jnp.pad(w_in_ab[e], ((0, 0), (0, LANES - n_heads))).astype(BF)
            bfg = jnp.pad(b_forget[e], (0, LANES - n_heads)).reshape(1, LANES)
            qg = row(jnp.tile(q_norm_g[e], n_heads))
            kg = row(jnp.tile(k_norm_g[e], n_heads))
            cw, cb = conv_w[e], row(conv_b[e])
            lg, lb = row(conv_ln_g[e]), row(conv_ln_b[e])
            (a_p, q_t, k_aug, v_t, kt_p, vt_p, lft_p, hist_p,
             q_s, k_s, v_s, glu_s, lf_s) = _even_in(
                xp, xs, g_mix, w_in, bfg, qg, kg, bd, cw, cb, lg, lb, sel,
                batch=batch, seq=seq)

            a_s = _sample_conv(glu_s, hist_t[e], cw, cb, lg, lb,
                               dec_batch=dec_batch, dec_seq=dec_seq)

            t_p = _flash(q_t, k_aug, v_t, batch=batch, seq=seq)

            lf_s4 = lf_s[:, :n_heads].reshape(dec_seq, dec_batch, n_heads)
            lfn_t = jnp.pad(jnp.transpose(lf_s4, (1, 2, 0)),
                            ((0, 0), (0, 0), (0, page - dec_seq)))
            tok = lambda a: a.reshape(dec_seq, dec_batch, 1, d_b)
            t_s = _decode(pt_flat, tok(q_s), tok(k_s), tok(v_s), lfn_t, ck_t, cv_t, clf_t,
                          layer=e, dec_batch=dec_batch, dec_seq=dec_seq).reshape(TM, d_b)

            wo = w_out_ab[e].astype(BF)
            xp, xs = _even_out(xp, xs, a_p, a_s, t_p, t_s, wo[:d_a], wo[d_a:], g_ffn, wu, wd)

            sample_major = lambda a: jnp.swapaxes(a.reshape(dec_seq, dec_batch, -1), 0, 1)
            outs["kp"].append(jnp.transpose(
                kt_p.reshape(batch, n_heads, head_dim, seq), (0, 3, 1, 2)))
            outs["vp"].append(jnp.transpose(
                vt_p.reshape(batch, n_heads, head_dim, seq), (0, 3, 1, 2)))
            outs["fp"].append(jnp.swapaxes(lft_p, 1, 2))
            outs["cp"].append(hist_p)
            outs["ks"].append(sample_major(k_s).reshape(dec_batch, dec_seq, n_heads, head_dim))
            outs["vs"].append(sample_major(v_s).reshape(dec_batch, dec_seq, n_heads, head_dim))
            outs["fs"].append(sample_major(lf_s[:, :n_heads]))
            outs["cs"].append(jnp.swapaxes(jnp.concatenate(
                [hist_t[e][dec_seq:], glu_s.reshape(dec_seq, dec_batch, d_a)], axis=0), 0, 1))
        else:
            o = layer // 2
            bs_full = jnp.repeat(b_spatial[o].T, d_c // n_groups, axis=1)
            ws_small = jnp.repeat(
                jnp.transpose(w_spatial[o][:, :dec_seq, :dec_seq], (1, 2, 0)),
                d_c // n_groups, axis=2).reshape(dec_seq * dec_seq, d_c)
            xp, xs, v_rows = _odd(
                xp, xs, g_mix, w_in_c[o].astype(BF), row(sgu_ln_g[o]), row(sgu_ln_b[o]),
                w_spatial[o].astype(BF), bs_full, ws_small, w_out_c[o].astype(BF),
                g_ffn, wu, wd, dec_batch=dec_batch, dec_seq=dec_seq)
            outs["chs"].append(jnp.swapaxes(v_rows.reshape(dec_seq, dec_batch, d_c), 0, 1))

    y_prompt = xp.reshape(batch, seq, d)
    y_sample = jnp.swapaxes(xs.reshape(dec_seq, dec_batch, d), 0, 1)
    return (y_prompt, y_sample,
            jnp.stack(outs["kp"]), jnp.stack(outs["vp"]), jnp.stack(outs["fp"]),
            jnp.stack(outs["ks"]), jnp.stack(outs["vs"]), jnp.stack(outs["fs"]),
            jnp.stack(outs["cp"]), jnp.stack(outs["cs"]), jnp.stack(outs["chs"]))
```

```python
import functools

import jax
import jax.numpy as jnp
from jax import lax
from jax.experimental import pallas as pl
from jax.experimental.pallas import tpu as pltpu

BF = jnp.bfloat16
F32 = jnp.float32

TM = 512
LANES = 128
SUBLANES = 8
N_HEADS = 8
HEAD_DIM = 64
PAIR = 2 * HEAD_DIM
CONV_WIDTH = 31
HIST = CONV_WIDTH - 1
HIST_PAD = 32
CHUNK = 128
RMS_EPS = 1e-6
LN_EPS = 1e-5
NEG = -1e30
VMEM_LIMIT = 56 * 1024 * 1024


def _dot(a, b):
    return jnp.dot(a, b, preferred_element_type=F32)


def _dot_nt(a, b):
    return lax.dot_general(a, b, (((1,), (1,)), ((), ())), preferred_element_type=F32)


def _rms(x, g):
    ms = jnp.mean(x * x, axis=-1, keepdims=True)
    return x * lax.rsqrt(ms + RMS_EPS) * g


def _ln(x, g, b):
    mu = jnp.mean(x, axis=-1, keepdims=True)
    xc = x - mu
    var = jnp.mean(xc * xc, axis=-1, keepdims=True)
    return xc * lax.rsqrt(var + LN_EPS) * g + b


def _log_sigmoid(z):
    return jnp.minimum(z, 0.0) - jnp.log1p(jnp.exp(-jnp.abs(z)))


def _split3(x):
    hi = x.astype(BF)
    r1 = x - hi.astype(F32)
    mid = r1.astype(BF)
    lo = (r1 - mid.astype(F32)).astype(BF)
    return hi, mid, lo


def _ffn(x, g_ref, wu_ref, wd_ref, chunk=1024):
    hn = _rms(x, g_ref[...]).astype(BF)
    acc = None
    for c in range(wu_ref.shape[1] // chunk):
        a = _dot(hn, wu_ref[:, c * chunk:(c + 1) * chunk])
        a = jnp.maximum(a, 0.0)
        part = _dot((a * a).astype(BF), wd_ref[c * chunk:(c + 1) * chunk, :])
        acc = part if acc is None else acc + part
    return x + acc


def _resident(shape):
    nd = len(shape)
    return pl.BlockSpec(shape, lambda *_: (0,) * nd, pipeline_mode=pl.Buffered(1))


def _layer_slab(stacked, layer, block=None, index=None):
    block = stacked.shape[1:] if block is None else block
    index = (0,) * len(block) if index is None else index
    return pl.BlockSpec((None,) + tuple(block), lambda *_: (layer,) + tuple(index),
                        pipeline_mode=pl.Buffered(1))


def _even_in_kernel(xp_ref, xs_ref, g_ref, w_ref, bfg_ref, qg_ref, kg_ref, bd_ref,
                    cw_ref, cb_ref, lg_ref, lb_ref, sel_ref, *rest,
                    n_ptiles, tiles_per_seq, n_aliased):
    (ap_ref, qt_ref, kaug_ref, vt_ref, ktp_ref, vtp_ref, lft_ref,
     histp_ref, qs_ref, ks_ref, vs_ref, glus_ref, lfs_ref,
     gbuf_ref, fcarry_ref, shift_ref) = rest[n_aliased:]
    i = pl.program_id(0)
    is_s = i >= n_ptiles
    is_p = jnp.logical_not(is_s)
    seq_first = jnp.logical_and(is_p, i % tiles_per_seq == 0)
    seq_last = jnp.logical_and(is_p, i % tiles_per_seq == tiles_per_seq - 1)
    d_a = cw_ref.shape[1]
    d_b = N_HEADS * HEAD_DIM

    x = jnp.where(is_s, xs_ref[...], xp_ref[...])
    h = _rms(x, g_ref[...]).astype(BF)

    a_in = _dot(h, w_ref[:, 0:2 * d_a])
    glu = a_in[:, :d_a] * jax.nn.sigmoid(a_in[:, d_a:])
    c0 = 2 * d_a
    q = _dot(h, w_ref[:, c0:c0 + d_b])
    k = _dot(h, w_ref[:, c0 + d_b:c0 + 2 * d_b])
    v = _dot(h, w_ref[:, c0 + 2 * d_b:c0 + 3 * d_b])
    fg = _dot(h, w_ref[:, c0 + 3 * d_b:c0 + 3 * d_b + LANES])

    bd = bd_ref[...]
    qn = q * lax.rsqrt(_dot((q * q).astype(BF), bd) + RMS_EPS) * qg_ref[...]
    kn = k * lax.rsqrt(_dot((k * k).astype(BF), bd) + RMS_EPS) * kg_ref[...]
    logf = _log_sigmoid(fg + bfg_ref[...])

    q_sc = qn * (HEAD_DIM ** -0.5)

    @pl.when(is_s)
    def _():
        qs_ref[...] = q_sc
        ks_ref[...] = kn
        vs_ref[...] = v
        glus_ref[...] = glu
        lfs_ref[...] = logf

    @pl.when(seq_first)
    def _():
        gbuf_ref[0:HIST_PAD, :] = jnp.zeros((HIST_PAD, d_a), F32)
        fcarry_ref[...] = jnp.zeros_like(fcarry_ref)

    @pl.when(is_p)
    def _():
        q_t = q_sc.T
        k_t = kn.T
        v_t = v.T
        ktp_ref[...] = k_t
        vtp_ref[...] = v_t
        lft_ref[...] = logf.T[0:N_HEADS, :]
        for hp in range(d_b // PAIR):
            sl = slice(hp * PAIR, (hp + 1) * PAIR)
            qt_ref[hp] = q_t[sl, :].astype(BF)
            vt_ref[hp] = v_t[sl, :].astype(BF)

        r = lax.broadcasted_iota(jnp.int32, (CHUNK, CHUNK), 0)
        c = lax.broadcasted_iota(jnp.int32, (CHUNK, CHUNK), 1)
        tril = (r >= c).astype(F32).astype(BF)
        hi, mid, lo = _split3(logf)
        carry = fcarry_ref[...]
        f_chunks = []
        for cb in range(TM // CHUNK):
            sl = slice(cb * CHUNK, (cb + 1) * CHUNK)
            fc = _dot(tril, hi[sl]) + _dot(tril, mid[sl]) + _dot(tril, lo[sl]) + carry
            f_chunks.append(fc)
            carry = fc[CHUNK - 1:CHUNK, :]
        fcarry_ref[...] = carry

        nf_hi, nf_mid, nf_lo = _split3(-jnp.concatenate(f_chunks, axis=0))
        lane = lax.broadcasted_iota(jnp.int32, (1, LANES), 1)
        parts = jnp.where(
            lane < N_HEADS, nf_hi.astype(F32),
            jnp.where(lane < 2 * N_HEADS, pltpu.roll(nf_mid.astype(F32), N_HEADS, axis=1),
                      pltpu.roll(nf_lo.astype(F32), 2 * N_HEADS, axis=1))).astype(BF)
        aug = _dot(parts, sel_ref[...])
        for hd in range(N_HEADS):
            pair = kn[:, (hd // 2) * PAIR:(hd // 2 + 1) * PAIR]
            if hd % 2:
                pair = pltpu.roll(pair, HEAD_DIM, axis=1)
            kaug_ref[hd] = jnp.where(lane < HEAD_DIM, pair,
                                     aug[:, hd * LANES:(hd + 1) * LANES]).astype(BF)

        gbuf_ref[HIST_PAD:HIST_PAD + TM, :] = glu
        acc = jnp.zeros((TM, d_a), F32)
        for b in range(SUBLANES):
            a_max = (CONV_WIDTH - 1 - b) // SUBLANES
            rows = TM + SUBLANES * a_max
            shift_ref[b % 2, 0:rows, :] = gbuf_ref[pl.ds(HIST_PAD - HIST + b, rows), :]
            for a in range(a_max + 1):
                t = SUBLANES * a + b
                acc = acc + cw_ref[t:t + 1, :] * shift_ref[b % 2, SUBLANES * a:SUBLANES * a + TM, :]
        y = _ln(acc + cb_ref[...], lg_ref[...], lb_ref[...])
        ap_ref[...] = (y * jax.nn.sigmoid(y)).astype(BF)
        gbuf_ref[0:HIST_PAD, :] = gbuf_ref[TM:TM + HIST_PAD, :]

    @pl.when(seq_last)
    def _():
        histp_ref[...] = gbuf_ref[pl.ds(HIST_PAD - HIST, HIST), :]


def _even_in(xp, xs, g, w_all, bfg, qg, kg, bd, cw, cb, lg, lb, sel, stacked, *,
             layer, n_layers, batch, seq):
    n_p, d = xp.shape
    d_a = cw.shape[1]
    d_b = N_HEADS * HEAD_DIM
    n_pairs = d_b // PAIR
    n_ptiles = n_p // TM
    tiles_per_seq = seq // TM
    last_p = n_ptiles - 1
    p_row = lambda i: (jnp.minimum(i, last_p), 0)
    p_col3 = lambda i: (0, 0, jnp.minimum(i, last_p))
    s_row = lambda i: (0, 0)
    seq_col = lambda i: (jnp.minimum(i, last_p) // tiles_per_seq, 0,
                         jnp.minimum(i, last_p) % tiles_per_seq)
    stacked = tuple(stacked)
    layer_col = lambda i: (layer,) + seq_col(i)
    kernel = functools.partial(_even_in_kernel, n_ptiles=n_ptiles, tiles_per_seq=tiles_per_seq,
                               n_aliased=len(stacked))
    out_shape = (
        jax.ShapeDtypeStruct((n_p, d_a), BF),
        jax.ShapeDtypeStruct((n_pairs, PAIR, n_p), BF),
        jax.ShapeDtypeStruct((N_HEADS, n_p, LANES), BF),
        jax.ShapeDtypeStruct((n_pairs, PAIR, n_p), BF),
        jax.ShapeDtypeStruct((n_layers, batch, d_b, seq), F32),
        jax.ShapeDtypeStruct((n_layers, batch, d_b, seq), F32),
        jax.ShapeDtypeStruct((n_layers, batch, N_HEADS, seq), F32),
        jax.ShapeDtypeStruct((batch, HIST, d_a), F32),
        jax.ShapeDtypeStruct((TM, d_b), F32),
        jax.ShapeDtypeStruct((TM, d_b), F32),
        jax.ShapeDtypeStruct((TM, d_b), F32),
        jax.ShapeDtypeStruct((TM, d_a), F32),
        jax.ShapeDtypeStruct((TM, LANES), F32),
    )
    out_specs = (
        pl.BlockSpec((TM, d_a), p_row),
        pl.BlockSpec((n_pairs, PAIR, TM), p_col3),
        pl.BlockSpec((N_HEADS, TM, LANES), lambda i: (0, jnp.minimum(i, last_p), 0)),
        pl.BlockSpec((n_pairs, PAIR, TM), p_col3),
        pl.BlockSpec((None, None, d_b, TM), layer_col),
        pl.BlockSpec((None, None, d_b, TM), layer_col),
        pl.BlockSpec((None, None, N_HEADS, TM), layer_col),
        pl.BlockSpec((None, HIST, d_a),
                     lambda i: (jnp.minimum(i // tiles_per_seq, batch - 1), 0, 0)),
        pl.BlockSpec((TM, d_b), s_row),
        pl.BlockSpec((TM, d_b), s_row),
        pl.BlockSpec((TM, d_b), s_row),
        pl.BlockSpec((TM, d_a), s_row),
        pl.BlockSpec((TM, LANES), s_row),
    )
    in_specs = [
        pl.BlockSpec((TM, d), p_row),
        _resident(xs.shape),
        _resident(g.shape), _layer_slab(w_all, layer), _resident(bfg.shape),
        _resident(qg.shape), _resident(kg.shape), _resident(bd.shape),
        _resident(cw.shape), _resident(cb.shape), _resident(lg.shape), _resident(lb.shape),
        _resident(sel.shape),
    ] + [pl.BlockSpec(memory_space=pl.ANY)] * len(stacked)
    first_stacked_out = 4
    return pl.pallas_call(
        kernel,
        grid=(n_ptiles + 1,),
        in_specs=in_specs,
        out_specs=out_specs,
        out_shape=out_shape,
        input_output_aliases={len(in_specs) - len(stacked) + j: first_stacked_out + j
                              for j in range(len(stacked))},
        scratch_shapes=[pltpu.VMEM((HIST_PAD + TM, d_a), F32),
                        pltpu.VMEM((1, LANES), F32),
                        pltpu.VMEM((2, HIST_PAD + TM, d_a), F32)],
        compiler_params=pltpu.CompilerParams(
            dimension_semantics=("arbitrary",), vmem_limit_bytes=VMEM_LIMIT),
        name="even_in",
    )(xp, xs, g, w_all, bfg, qg, kg, bd, cw, cb, lg, lb, sel, *stacked)


def _sample_conv_kernel(glu_ref, hist_ref, cw_ref, cb_ref, lg_ref, lb_ref, a_ref,
                        *, dec_batch, dec_seq):
    d_a = cw_ref.shape[1]

    def full(j):
        if j < HIST:
            return hist_ref[j]
        return glu_ref[(j - HIST) * dec_batch:(j - HIST + 1) * dec_batch, :]

    for t in range(dec_seq):
        acc = jnp.zeros((dec_batch, d_a), F32)
        for kk in range(CONV_WIDTH):
            acc = acc + cw_ref[kk:kk + 1, :] * full(t + kk)
        y = _ln(acc + cb_ref[...], lg_ref[...], lb_ref[...])
        a_ref[t * dec_batch:(t + 1) * dec_batch, :] = (y * jax.nn.sigmoid(y)).astype(BF)


def _sample_conv(glu_s, hist2d, cw, cb, lg, lb, *, dec_batch, dec_seq):
    kernel = functools.partial(_sample_conv_kernel, dec_batch=dec_batch, dec_seq=dec_seq)
    return pl.pallas_call(
        kernel,
        out_shape=jax.ShapeDtypeStruct(glu_s.shape, BF),
        compiler_params=pltpu.CompilerParams(vmem_limit_bytes=VMEM_LIMIT),
        name="sample_conv",
    )(glu_s, hist2d, cw, cb, lg, lb)


def _flash_kernel(qt_ref, ka0_ref, ka1_ref, vt_ref, o_ref, s_ref, *, tq):
    qi = pl.program_id(2)
    q0 = pl.multiple_of(qi * tq, tq)
    q2 = qt_ref[...].astype(F32)
    row = lax.broadcasted_iota(jnp.int32, (PAIR, 1), 0)
    ones_rows = jnp.where(jnp.logical_and(row >= HEAD_DIM, row < HEAD_DIM + 3), 1.0, 0.0)
    swapped = jnp.concatenate([q2[HEAD_DIM:], q2[:HEAD_DIM]], axis=0)
    qa = tuple(jnp.where(row < HEAD_DIM, qq, ones_rows).astype(BF) for qq in (q2, swapped))
    ka = (ka0_ref, ka1_ref)

    def scores(ki, slot):
        start = pl.multiple_of(ki * tq, tq)
        for hh in range(2):
            s_ref[slot, hh] = _dot(ka[hh][pl.ds(start, tq), :], qa[hh])

    def tile(ki, slot, carry, masked):
        start = pl.multiple_of(ki * tq, tq)
        m, l, acc = carry
        new_m, new_l, new_acc = [], [], []
        for hh in range(2):
            s_t = s_ref[slot, hh]
            if masked:
                r = lax.broadcasted_iota(jnp.int32, (tq, tq), 0)
                c = lax.broadcasted_iota(jnp.int32, (tq, tq), 1)
                s_t = jnp.where(r <= c, s_t, NEG)
            m_new = jnp.maximum(m[hh], jnp.max(s_t, axis=0, keepdims=True))
            alpha = jnp.exp(m[hh] - m_new)
            p = jnp.exp(s_t - m_new)
            new_l.append(alpha * l[hh] + jnp.sum(p, axis=0, keepdims=True))
            new_m.append(m_new)
            v_t = vt_ref[hh * HEAD_DIM:(hh + 1) * HEAD_DIM, pl.ds(start, tq)]
            new_acc.append(alpha * acc[hh] + _dot(v_t, p.astype(BF)))
        return tuple(new_m), tuple(new_l), tuple(new_acc)

    neg = jnp.full((1, tq), NEG, F32)
    zrow = jnp.zeros((1, tq), F32)
    zacc = jnp.zeros((HEAD_DIM, tq), F32)
    init = ((neg, neg), (zrow, zrow), (zacc, zacc))

    def finish(carry):
        _, l, acc = carry
        out_t = jnp.concatenate([acc[0] / l[0], acc[1] / l[1]], axis=0)
        o_ref[...] = out_t.T.astype(BF)

    def pair(j, carry):
        scores(2 * j + 1, 1)
        carry = tile(2 * j, 0, carry, False)
        scores(2 * j + 2, 0)
        return tile(2 * j + 1, 1, carry, False)

    scores(0, 0)
    carry = lax.fori_loop(0, qi // 2, pair, init)

    @pl.when(qi % 2 == 0)
    def _():
        finish(tile(qi, 0, carry, True))

    @pl.when(qi % 2 == 1)
    def _():
        scores(qi, 1)
        finish(tile(qi, 1, tile(qi - 1, 0, carry, False), True))


def _flash(q_t, k_aug, v_t, *, batch, seq):
    n_pairs = q_t.shape[0]
    tq = TM
    nq = seq // tq
    kernel = functools.partial(_flash_kernel, tq=tq)
    t_spec = pl.BlockSpec((None, PAIR, seq), lambda b, hp, qi: (hp, 0, b))
    return pl.pallas_call(
        kernel,
        grid=(batch, n_pairs, nq),
        in_specs=[
            pl.BlockSpec((None, PAIR, tq), lambda b, hp, qi: (hp, 0, b * nq + qi)),
            pl.BlockSpec((None, seq, LANES), lambda b, hp, qi: (2 * hp, b, 0)),
            pl.BlockSpec((None, seq, LANES), lambda b, hp, qi: (2 * hp + 1, b, 0)),
            t_spec,
        ],
        out_specs=pl.BlockSpec((tq, PAIR), lambda b, hp, qi: (b * nq + qi, hp)),
        out_shape=jax.ShapeDtypeStruct((batch * seq, n_pairs * PAIR), BF),
        scratch_shapes=[pltpu.VMEM((2, 2, tq, tq), F32)],
        compiler_params=pltpu.CompilerParams(
            dimension_semantics=("arbitrary", "arbitrary", "arbitrary"),
            vmem_limit_bytes=VMEM_LIMIT),
        name="flash_prompt",
    )(q_t, k_aug, k_aug, v_t)


def _decode_kernel(pt_ref, q_ref, kn_ref, vn_ref, lfn_ref, *rest, n_pages, dec_seq, page):
    del pt_ref
    k_refs = rest[:n_pages]
    v_refs = rest[n_pages:2 * n_pages]
    lf_refs = rest[2 * n_pages:3 * n_pages]
    o_ref = rest[3 * n_pages]
    kx_ref, vx_ref = rest[3 * n_pages + 1:]
    d_b = N_HEADS * HEAD_DIM
    rows = dec_seq * N_HEADS

    @pl.when(pl.program_id(0) == 0)
    def _():
        kx_ref[...] = jnp.zeros_like(kx_ref)
        vx_ref[...] = jnp.zeros_like(vx_ref)

    kx_ref[0:dec_seq, :] = kn_ref[:, 0, :]
    vx_ref[0:dec_seq, :] = vn_ref[:, 0, :]

    lane_h = lax.broadcasted_iota(jnp.int32, (N_HEADS, d_b), 1) // HEAD_DIM
    row_h = lax.broadcasted_iota(jnp.int32, (N_HEADS, d_b), 0)
    head_mask = lane_h == row_h
    q4 = q_ref[:, 0, :].astype(F32)
    q32 = jnp.where(head_mask[None], q4[:, None, :], 0.0).reshape(rows, d_b).astype(BF)

    lf = jnp.concatenate([r[...] for r in lf_refs], axis=0)
    lane = lax.broadcasted_iota(jnp.int32, lf.shape, 1)
    sfx = lf
    d = 1
    while d < page:
        sfx = sfx + jnp.where(lane < page - d, pltpu.roll(sfx, page - d, axis=1), 0.0)
        d *= 2
    excl = sfx - lf
    tot = sfx[:, 0:1]
    bias = [None] * n_pages
    run = jnp.zeros((N_HEADS, 1), F32)
    for p in range(n_pages - 1, -1, -1):
        sl = slice(p * N_HEADS, (p + 1) * N_HEADS)
        bias[p] = excl[sl] + run
        run = run + tot[sl]

    s_pages = []
    for p in range(n_pages):
        k_t = k_refs[p][...].reshape(d_b, page).astype(BF)
        s = _dot(q32, k_t).reshape(dec_seq, N_HEADS, page)
        s_pages.append(s + bias[p][None])

    lfn = lfn_ref[...]
    lane8 = lax.broadcasted_iota(jnp.int32, lfn.shape, 1)
    cs = lfn
    d = 1
    while d < dec_seq:
        cs = cs + jnp.where(lane8 >= d, pltpu.roll(cs, d, axis=1), 0.0)
        d *= 2
    s_new = _dot_nt(q32, kx_ref[...].astype(BF)).reshape(dec_seq, N_HEADS, page) - cs[None]
    step = lax.broadcasted_iota(jnp.int32, (dec_seq, N_HEADS, page), 0)
    key = lax.broadcasted_iota(jnp.int32, (dec_seq, N_HEADS, page), 2)
    s_new = jnp.where(key <= step, s_new, NEG)

    mx = s_new
    for s in s_pages:
        mx = jnp.maximum(mx, s)
    m = jnp.max(mx, axis=2, keepdims=True)
    p_new = jnp.exp(s_new - m)
    psum = p_new
    o = _dot(p_new.reshape(rows, page).astype(BF), vx_ref[...].astype(BF))
    for p in range(n_pages):
        pp = jnp.exp(s_pages[p] - m)
        psum = psum + pp
        v_t = v_refs[p][...].reshape(d_b, page).astype(BF)
        o = o + _dot_nt(pp.reshape(rows, page).astype(BF), v_t)
    l = jnp.sum(psum, axis=2, keepdims=True)
    o = o.reshape(dec_seq, N_HEADS, d_b) / l
    o_ref[:, 0, :] = jnp.sum(jnp.where(head_mask[None], o, 0.0), axis=1).astype(BF)


def _decode(page_table_flat, q_s, k_s, v_s, lfn_t, cache_kt, cache_vt, cache_lf_t, *,
            layer, dec_batch, dec_seq):
    n_pages = page_table_flat.shape[0] // dec_batch
    page = cache_kt.shape[4]
    d_b = N_HEADS * HEAD_DIM
    tok_spec = pl.BlockSpec((dec_seq, None, 1, d_b), lambda b, pt: (0, b, 0, 0))

    def page_map(p, tail, b, pt):
        return (layer, pt[b * n_pages + p]) + (0,) * tail

    kv_specs = [pl.BlockSpec((None, None, N_HEADS, HEAD_DIM, page),
                             functools.partial(page_map, p, 3)) for p in range(n_pages)]
    lf_specs = [pl.BlockSpec((None, None, N_HEADS, page),
                             functools.partial(page_map, p, 2)) for p in range(n_pages)]
    kernel = functools.partial(_decode_kernel, n_pages=n_pages, dec_seq=dec_seq, page=page)
    grid_spec = pltpu.PrefetchScalarGridSpec(
        num_scalar_prefetch=1,
        grid=(dec_batch,),
        in_specs=[tok_spec, tok_spec, tok_spec,
                  pl.BlockSpec((None, N_HEADS, page), lambda b, pt: (b, 0, 0))]
        + kv_specs + kv_specs + lf_specs,
        out_specs=tok_spec,
        scratch_shapes=[pltpu.VMEM((page, d_b), F32), pltpu.VMEM((page, d_b), F32)],
    )
    return pl.pallas_call(
        kernel,
        grid_spec=grid_spec,
        out_shape=jax.ShapeDtypeStruct((dec_seq, dec_batch, 1, d_b), BF),
        compiler_params=pltpu.CompilerParams(
            dimension_semantics=("arbitrary",), vmem_limit_bytes=VMEM_LIMIT),
        name="decode_attn",
    )(page_table_flat, q_s, k_s, v_s, lfn_t,
      *([cache_kt] * n_pages), *([cache_vt] * n_pages), *([cache_lf_t] * n_pages))


def _even_out_kernel(xp_ref, xs_ref, ap_ref, as_ref, tp_ref, ts_ref, woa_ref, wob_ref,
                     g_ref, wu_ref, wd_ref, yp_ref, ys_ref, *, n_ptiles):
    i = pl.program_id(0)
    is_s = i >= n_ptiles
    x = jnp.where(is_s, xs_ref[...], xp_ref[...])
    a = jnp.where(is_s, as_ref[...], ap_ref[...])
    t = jnp.where(is_s, ts_ref[...], tp_ref[...])
    x = x + (_dot(a, woa_ref[...]) + _dot(t, wob_ref[...]))
    y = _ffn(x, g_ref, wu_ref, wd_ref)

    @pl.when(is_s)
    def _():
        ys_ref[...] = y

    @pl.when(jnp.logical_not(is_s))
    def _():
        yp_ref[...] = y


def _even_out(xp, xs, a_p, a_s, t_p, t_s, wo_all, g, wu_all, wd_all, *, e, layer):
    n_p, d = xp.shape
    d_a = a_p.shape[1]
    d_b = t_p.shape[1]
    n_ptiles = n_p // TM
    last_p = n_ptiles - 1
    p_row = lambda i: (jnp.minimum(i, last_p), 0)
    kernel = functools.partial(_even_out_kernel, n_ptiles=n_ptiles)
    assert d_a == d_b
    return pl.pallas_call(
        kernel,
        grid=(n_ptiles + 1,),
        in_specs=[
            pl.BlockSpec((TM, d), p_row), _resident(xs.shape),
            pl.BlockSpec((TM, d_a), p_row), _resident(a_s.shape),
            pl.BlockSpec((TM, d_b), p_row), _resident(t_s.shape),
            _layer_slab(wo_all, e, (d_a, d), (0, 0)), _layer_slab(wo_all, e, (d_b, d), (1, 0)),
            _resident(g.shape), _layer_slab(wu_all, layer), _layer_slab(wd_all, layer),
        ],
        out_specs=(pl.BlockSpec((TM, d), p_row), pl.BlockSpec((TM, d), lambda i: (0, 0))),
        out_shape=(jax.ShapeDtypeStruct(xp.shape, F32), jax.ShapeDtypeStruct(xs.shape, F32)),
        compiler_params=pltpu.CompilerParams(
            dimension_semantics=("arbitrary",), vmem_limit_bytes=VMEM_LIMIT),
        name="even_out",
    )(xp, xs, a_p, a_s, t_p, t_s, wo_all, wo_all, g, wu_all, wd_all)


def _odd_kernel(xp_ref, xs_ref, gm_ref, wi_ref, lg_ref, lb_ref, ws_ref, bs_ref, wsm_ref,
                wo_ref, gf_ref, wu_ref, wd_ref, yp_ref, ys_ref, vs_ref, mix_ref,
                *, n_ptiles, dec_batch, dec_seq):
    i = pl.program_id(0)
    is_s = i >= n_ptiles
    d_c = wo_ref.shape[0]
    n_groups = ws_ref.shape[0]
    c_hd = d_c // n_groups

    x = jnp.where(is_s, xs_ref[...], xp_ref[...])
    h = _rms(x, gm_ref[...]).astype(BF)
    u = jax.nn.gelu(_dot(h, wi_ref[:, :d_c]))
    v = _ln(jax.nn.gelu(_dot(h, wi_ref[:, d_c:])), lg_ref[...], lb_ref[...])
    v_b = v.astype(BF)

    @pl.when(jnp.logical_not(is_s))
    def _():
        r = lax.broadcasted_iota(jnp.int32, (CHUNK, CHUNK), 0)
        c = lax.broadcasted_iota(jnp.int32, (CHUNK, CHUNK), 1)
        for hg in range(n_groups):
            w = jnp.where(r >= c, ws_ref[hg], jnp.zeros((CHUNK, CHUNK), BF))
            cols = slice(hg * c_hd, (hg + 1) * c_hd)
            for cb in range(TM // CHUNK):
                rows = slice(cb * CHUNK, (cb + 1) * CHUNK)
                mix_ref[rows, cols] = _dot(w, v_b[rows, cols]) + bs_ref[:, cols]

    @pl.when(is_s)
    def _():
        vs_ref[...] = v
        v_r = v_b.astype(F32)
        for t in range(dec_seq):
            acc = jnp.zeros((dec_batch, d_c), F32) + bs_ref[t:t + 1, :]
            for j in range(t + 1):
                wt = wsm_ref[t * dec_seq + j:t * dec_seq + j + 1, :].astype(BF).astype(F32)
                acc = acc + wt * v_r[j * dec_batch:(j + 1) * dec_batch, :]
            mix_ref[t * dec_batch:(t + 1) * dec_batch, :] = acc

    gte = (u * mix_ref[...]).astype(BF)
    x = x + _dot(gte, wo_ref[...])
    y = _ffn(x, gf_ref, wu_ref, wd_ref)

    @pl.when(is_s)
    def _():
        ys_ref[...] = y

    @pl.when(jnp.logical_not(is_s))
    def _():
        yp_ref[...] = y


def _odd(xp, xs, gm, wi_all, lg, lb, ws_all, bs_full, ws_small, wo_all, gf, wu_all, wd_all, *,
         o, layer, dec_batch, dec_seq):
    n_p, d = xp.shape
    d_c = wo_all.shape[1]
    n_ptiles = n_p // TM
    last_p = n_ptiles - 1
    p_row = lambda i: (jnp.minimum(i, last_p), 0)
    s_row = lambda i: (0, 0)
    kernel = functools.partial(_odd_kernel, n_ptiles=n_ptiles, dec_batch=dec_batch,
                               dec_seq=dec_seq)
    return pl.pallas_call(
        kernel,
        grid=(n_ptiles + 1,),
        in_specs=[
            pl.BlockSpec((TM, d), p_row), _resident(xs.shape),
            _resident(gm.shape), _layer_slab(wi_all, o), _resident(lg.shape), _resident(lb.shape),
            _layer_slab(ws_all, o), _resident(bs_full.shape), _resident(ws_small.shape),
            _layer_slab(wo_all, o), _resident(gf.shape),
            _layer_slab(wu_all, layer), _layer_slab(wd_all, layer),
        ],
        out_specs=(pl.BlockSpec((TM, d), p_row), pl.BlockSpec((TM, d), s_row),
                   pl.BlockSpec((TM, d_c), s_row)),
        out_shape=(jax.ShapeDtypeStruct(xp.shape, F32), jax.ShapeDtypeStruct(xs.shape, F32),
                   jax.ShapeDtypeStruct((TM, d_c), F32)),
        scratch_shapes=[pltpu.VMEM((TM, d_c), F32)],
        compiler_params=pltpu.CompilerParams(
            dimension_semantics=("arbitrary",), vmem_limit_bytes=VMEM_LIMIT),
        name="odd_layer",
    )(xp, xs, gm, wi_all, lg, lb, ws_all, bs_full, ws_small, wo_all, gf, wu_all, wd_all)


def kernel(x_prompt, x_sample, cache_k, cache_v, cache_logf, state_conv, page_table,
           norm_mix_g, norm_ffn_g, w_in_ab, b_forget, q_norm_g, k_norm_g,
           conv_w, conv_b, conv_ln_g, conv_ln_b, w_out_ab,
           w_in_c, sgu_ln_g, sgu_ln_b, w_spatial, b_spatial, w_out_c,
           w_ff_up, w_ff_down):
    batch, seq, d = x_prompt.shape
    dec_batch, dec_seq, _ = x_sample.shape
    depth = norm_mix_g.shape[0]
    n_ab, n_pool, page, n_heads, head_dim = cache_k.shape
    d_b = n_heads * head_dim
    d_a = conv_w.shape[2]
    n_groups = w_spatial.shape[1]
    d_c = w_out_c.shape[1]
    assert dec_batch * dec_seq == TM and seq % TM == 0 and page == LANES
    assert n_heads == N_HEADS and head_dim == HEAD_DIM and conv_w.shape[1] == CONV_WIDTH
    assert w_spatial.shape[2] == CHUNK and dec_seq <= SUBLANES

    xp = x_prompt.reshape(batch * seq, d)
    xs = jnp.swapaxes(x_sample, 0, 1).reshape(TM, d)

    row = lambda a: a.reshape(1, -1).astype(F32)
    ck_t = jnp.transpose(cache_k, (0, 1, 3, 4, 2))
    cv_t = jnp.transpose(cache_v, (0, 1, 3, 4, 2))
    clf_t = jnp.swapaxes(cache_logf, 2, 3)
    hist_t = jnp.swapaxes(state_conv, 1, 2)
    pt_flat = page_table.reshape(-1)
    head_of = jnp.arange(d_b) // head_dim
    bd = (head_of[:, None] == head_of[None, :]).astype(BF) * (1.0 / head_dim)
    src = jnp.arange(LANES)
    dst = (src % n_heads) * LANES + head_dim + src // n_heads
    sel = jnp.logical_and(src[:, None] < 3 * n_heads,
                          jnp.arange(n_heads * LANES)[None, :] == dst[:, None]).astype(BF)

    wu_all = w_ff_up.astype(BF)
    wd_all = w_ff_down.astype(BF)
    w_in_all = jnp.pad(w_in_ab, ((0, 0), (0, 0), (0, LANES - n_heads))).astype(BF)
    wo_ab_all = w_out_ab.astype(BF)
    wi_c_all = w_in_c.astype(BF)
    wo_c_all = w_out_c.astype(BF)
    ws_all = w_spatial.astype(BF)

    outs = {k: [] for k in ("cp", "ks", "vs", "fs", "cs", "chs")}
    stacked = (jnp.zeros((n_ab, batch, d_b, seq), F32), jnp.zeros((n_ab, batch, d_b, seq), F32),
               jnp.zeros((n_ab, batch, n_heads, seq), F32))
    for layer in range(depth):
        g_mix = row(norm_mix_g[layer])
        g_ffn = row(norm_ffn_g[layer])
        if layer % 2 == 0:
            e = layer // 2
            bfg = jnp.pad(b_forget[e], (0, LANES - n_heads)).reshape(1, LANES)
            qg = row(jnp.tile(q_norm_g[e], n_heads))
            kg = row(jnp.tile(k_norm_g[e], n_heads))
            cw, cb = conv_w[e], row(conv_b[e])
            lg, lb = row(conv_ln_g[e]), row(conv_ln_b[e])
            (a_p, q_t, k_aug, v_t, kt_all, vt_all, lft_all, hist_p,
             q_s, k_s, v_s, glu_s, lf_s) = _even_in(
                xp, xs, g_mix, w_in_all, bfg, qg, kg, bd, cw, cb, lg, lb, sel, stacked,
                layer=e, n_layers=n_ab, batch=batch, seq=seq)
            stacked = (kt_all, vt_all, lft_all)

            a_s = _sample_conv(glu_s, hist_t[e], cw, cb, lg, lb,
                               dec_batch=dec_batch, dec_seq=dec_seq)

            t_p = _flash(q_t, k_aug, v_t, batch=batch, seq=seq)

            lf_s4 = lf_s[:, :n_heads].reshape(dec_seq, dec_batch, n_heads)
            lfn_t = jnp.pad(jnp.transpose(lf_s4, (1, 2, 0)),
                            ((0, 0), (0, 0), (0, page - dec_seq)))
            tok = lambda a: a.reshape(dec_seq, dec_batch, 1, d_b)
            t_s = _decode(pt_flat, tok(q_s), tok(k_s), tok(v_s), lfn_t, ck_t, cv_t, clf_t,
                          layer=e, dec_batch=dec_batch, dec_seq=dec_seq).reshape(TM, d_b)

            xp, xs = _even_out(xp, xs, a_p, a_s, t_p, t_s, wo_ab_all, g_ffn, wu_all, wd_all,
                               e=e, layer=layer)

            sample_major = lambda a: jnp.swapaxes(a.reshape(dec_seq, dec_batch, -1), 0, 1)
            outs["cp"].append(hist_p)
            outs["ks"].append(sample_major(k_s).reshape(dec_batch, dec_seq, n_heads, head_dim))
            outs["vs"].append(sample_major(v_s).reshape(dec_batch, dec_seq, n_heads, head_dim))
            outs["fs"].append(sample_major(lf_s[:, :n_heads]))
            outs["cs"].append(jnp.swapaxes(jnp.concatenate(
                [hist_t[e][dec_seq:], glu_s.reshape(dec_seq, dec_batch, d_a)], axis=0), 0, 1))
        else:
            o = layer // 2
            bs_full = jnp.repeat(b_spatial[o].T, d_c // n_groups, axis=1)
            ws_small = jnp.repeat(
                jnp.transpose(w_spatial[o][:, :dec_seq, :dec_seq], (1, 2, 0)),
                d_c // n_groups, axis=2).reshape(dec_seq * dec_seq, d_c)
            xp, xs, v_rows = _odd(
                xp, xs, g_mix, wi_c_all, row(sgu_ln_g[o]), row(sgu_ln_b[o]),
                ws_all, bs_full, ws_small, wo_c_all, g_ffn, wu_all, wd_all,
                o=o, layer=layer, dec_batch=dec_batch, dec_seq=dec_seq)
            outs["chs"].append(jnp.swapaxes(v_rows.reshape(dec_seq, dec_batch, d_c), 0, 1))

    y_prompt = xp.reshape(batch, seq, d)
    y_sample = jnp.swapaxes(xs.reshape(dec_seq, dec_batch, d), 0, 1)
    kt_all, vt_all, lft_all = stacked
    to_seq_major = lambda a: jnp.transpose(
        a.reshape(n_ab, batch, n_heads, head_dim, seq), (0, 1, 4, 2, 3))
    return (y_prompt, y_sample,
            to_seq_major(kt_all), to_seq_major(vt_all), jnp.swapaxes(lft_all, 2, 3),
            jnp.stack(outs["ks"]), jnp.stack(outs["vs"]), jnp.stack(outs["fs"]),
            jnp.stack(outs["cp"]), jnp.stack(outs["cs"]), jnp.stack(outs["chs"]))
```

```python
import functools

import jax
import jax.numpy as jnp
from jax import lax
from jax.experimental import pallas as pl
from jax.experimental.pallas import tpu as pltpu

BF = jnp.bfloat16
F32 = jnp.float32

TM = 512
LANES = 128
SUBLANES = 8
N_HEADS = 8
HEAD_DIM = 64
PAIR = 2 * HEAD_DIM
CONV_WIDTH = 31
HIST = CONV_WIDTH - 1
HIST_PAD = 32
CHUNK = 128
RMS_EPS = 1e-6
LN_EPS = 1e-5
NEG = -1e30
VMEM_LIMIT = 56 * 1024 * 1024


def _dot(a, b):
    return jnp.dot(a, b, preferred_element_type=F32)


def _dot_nt(a, b):
    return lax.dot_general(a, b, (((1,), (1,)), ((), ())), preferred_element_type=F32)


def _rms(x, g):
    ms = jnp.mean(x * x, axis=-1, keepdims=True)
    return x * lax.rsqrt(ms + RMS_EPS) * g


def _ln(x, g, b):
    mu = jnp.mean(x, axis=-1, keepdims=True)
    xc = x - mu
    var = jnp.mean(xc * xc, axis=-1, keepdims=True)
    return xc * lax.rsqrt(var + LN_EPS) * g + b


def _log_sigmoid(z):
    return jnp.minimum(z, 0.0) - jnp.log1p(jnp.exp(-jnp.abs(z)))


def _split3(x):
    hi = x.astype(BF)
    r1 = x - hi.astype(F32)
    mid = r1.astype(BF)
    lo = (r1 - mid.astype(F32)).astype(BF)
    return hi, mid, lo


def _ffn(x, g_ref, wu_ref, wd_ref, chunk=1024):
    hn = _rms(x, g_ref[...]).astype(BF)
    acc = None
    for c in range(wu_ref.shape[1] // chunk):
        a = _dot(hn, wu_ref[:, c * chunk:(c + 1) * chunk])
        a = jnp.maximum(a, 0.0)
        part = _dot((a * a).astype(BF), wd_ref[c * chunk:(c + 1) * chunk, :])
        acc = part if acc is None else acc + part
    return x + acc


def _resident(shape):
    nd = len(shape)
    return pl.BlockSpec(shape, lambda *_: (0,) * nd, pipeline_mode=pl.Buffered(1))


def _layer_slab(stacked, layer, block=None, index=None):
    block = stacked.shape[1:] if block is None else block
    index = (0,) * len(block) if index is None else index
    return pl.BlockSpec((None,) + tuple(block), lambda *_: (layer,) + tuple(index),
                        pipeline_mode=pl.Buffered(1))


def _even_in_kernel(xp_ref, xs_ref, g_ref, w_ref, bfg_ref, qg_ref, kg_ref, bd_ref,
                    cw_ref, cb_ref, lg_ref, lb_ref, sel_ref, *rest,
                    n_ptiles, tiles_per_seq, n_aliased):
    (ap_ref, qt_ref, kaug_ref, vt_ref, ktp_ref, vtp_ref, lft_ref,
     histp_ref, qs_ref, ks_ref, vs_ref, glus_ref, lfs_ref,
     gbuf_ref, fcarry_ref, shift_ref) = rest[n_aliased:]
    i = pl.program_id(0)
    is_s = i >= n_ptiles
    is_p = jnp.logical_not(is_s)
    seq_first = jnp.logical_and(is_p, i % tiles_per_seq == 0)
    seq_last = jnp.logical_and(is_p, i % tiles_per_seq == tiles_per_seq - 1)
    d_a = cw_ref.shape[1]
    d_b = N_HEADS * HEAD_DIM

    def project(x):
        h = _rms(x, g_ref[...]).astype(BF)
        a_in = _dot(h, w_ref[:, 0:2 * d_a])
        glu = a_in[:, :d_a] * jax.nn.sigmoid(a_in[:, d_a:])
        c0 = 2 * d_a
        q = _dot(h, w_ref[:, c0:c0 + d_b])
        k = _dot(h, w_ref[:, c0 + d_b:c0 + 2 * d_b])
        v = _dot(h, w_ref[:, c0 + 2 * d_b:c0 + 3 * d_b])
        fg = _dot(h, w_ref[:, c0 + 3 * d_b:c0 + 3 * d_b + LANES])
        bd = bd_ref[...]
        qn = q * lax.rsqrt(_dot((q * q).astype(BF), bd) + RMS_EPS) * qg_ref[...]
        kn = k * lax.rsqrt(_dot((k * k).astype(BF), bd) + RMS_EPS) * kg_ref[...]
        logf = _log_sigmoid(fg + bfg_ref[...])
        q_sc = qn * (HEAD_DIM ** -0.5)
        return glu, q_sc, kn, v, logf

    @pl.when(is_s)
    def _():
        glu, q_sc, kn, v, logf = project(xs_ref[...])
        qs_ref[...] = q_sc
        ks_ref[...] = kn
        vs_ref[...] = v
        glus_ref[...] = glu
        lfs_ref[...] = logf

    @pl.when(seq_first)
    def _():
        gbuf_ref[0:HIST_PAD, :] = jnp.zeros((HIST_PAD, d_a), F32)
        fcarry_ref[...] = jnp.zeros_like(fcarry_ref)

    @pl.when(is_p)
    def _():
        glu, q_sc, kn, v, logf = project(xp_ref[...])
        q_t = q_sc.T
        k_t = kn.T
        v_t = v.T
        ktp_ref[...] = k_t
        vtp_ref[...] = v_t
        lft_ref[...] = logf.T[0:N_HEADS, :]
        for hp in range(d_b // PAIR):
            sl = slice(hp * PAIR, (hp + 1) * PAIR)
            qt_ref[hp] = q_t[sl, :].astype(BF)
            vt_ref[hp] = v_t[sl, :].astype(BF)

        r = lax.broadcasted_iota(jnp.int32, (CHUNK, CHUNK), 0)
        c = lax.broadcasted_iota(jnp.int32, (CHUNK, CHUNK), 1)
        tril = (r >= c).astype(F32).astype(BF)
        hi, mid, lo = _split3(logf)
        carry = fcarry_ref[...]
        f_chunks = []
        for cb in range(TM // CHUNK):
            sl = slice(cb * CHUNK, (cb + 1) * CHUNK)
            fc = _dot(tril, hi[sl]) + _dot(tril, mid[sl]) + _dot(tril, lo[sl]) + carry
            f_chunks.append(fc)
            carry = fc[CHUNK - 1:CHUNK, :]
        fcarry_ref[...] = carry

        nf_hi, nf_mid, nf_lo = _split3(-jnp.concatenate(f_chunks, axis=0))
        lane = lax.broadcasted_iota(jnp.int32, (1, LANES), 1)
        parts = jnp.where(
            lane < N_HEADS, nf_hi.astype(F32),
            jnp.where(lane < 2 * N_HEADS, pltpu.roll(nf_mid.astype(F32), N_HEADS, axis=1),
                      pltpu.roll(nf_lo.astype(F32), 2 * N_HEADS, axis=1))).astype(BF)
        aug = _dot(parts, sel_ref[...])
        for hd in range(N_HEADS):
            pair = kn[:, (hd // 2) * PAIR:(hd // 2 + 1) * PAIR]
            if hd % 2:
                pair = pltpu.roll(pair, HEAD_DIM, axis=1)
            kaug_ref[hd] = jnp.where(lane < HEAD_DIM, pair,
                                     aug[:, hd * LANES:(hd + 1) * LANES]).astype(BF)

        gbuf_ref[HIST_PAD:HIST_PAD + TM, :] = glu
        acc = jnp.zeros((TM, d_a), F32)
        for b in range(SUBLANES):
            a_max = (CONV_WIDTH - 1 - b) // SUBLANES
            rows = TM + SUBLANES * a_max
            shift_ref[b % 2, 0:rows, :] = gbuf_ref[pl.ds(HIST_PAD - HIST + b, rows), :]
            for a in range(a_max + 1):
                t = SUBLANES * a + b
                acc = acc + cw_ref[t:t + 1, :] * shift_ref[b % 2, SUBLANES * a:SUBLANES * a + TM, :]
        y = _ln(acc + cb_ref[...], lg_ref[...], lb_ref[...])
        ap_ref[...] = (y * jax.nn.sigmoid(y)).astype(BF)
        gbuf_ref[0:HIST_PAD, :] = gbuf_ref[TM:TM + HIST_PAD, :]

    @pl.when(seq_last)
    def _():
        histp_ref[...] = gbuf_ref[pl.ds(HIST_PAD - HIST, HIST), :]


def _even_in(xp, xs, g, w_all, bfg, qg, kg, bd, cw, cb, lg, lb, sel, stacked, *,
             layer, n_layers, batch, seq):
    n_p, d = xp.shape
    d_a = cw.shape[1]
    d_b = N_HEADS * HEAD_DIM
    n_pairs = d_b // PAIR
    n_ptiles = n_p // TM
    tiles_per_seq = seq // TM
    last_p = n_ptiles - 1
    p_row = lambda i: (jnp.minimum(i, last_p), 0)
    p_col3 = lambda i: (0, 0, jnp.minimum(i, last_p))
    s_row = lambda i: (0, 0)
    seq_col = lambda i: (jnp.minimum(i, last_p) // tiles_per_seq, 0,
                         jnp.minimum(i, last_p) % tiles_per_seq)
    stacked = tuple(stacked)
    layer_col = lambda i: (layer,) + seq_col(i)
    kernel = functools.partial(_even_in_kernel, n_ptiles=n_ptiles, tiles_per_seq=tiles_per_seq,
                               n_aliased=len(stacked))
    out_shape = (
        jax.ShapeDtypeStruct((n_p, d_a), BF),
        jax.ShapeDtypeStruct((n_pairs, PAIR, n_p), BF),
        jax.ShapeDtypeStruct((N_HEADS, n_p, LANES), BF),
        jax.ShapeDtypeStruct((n_pairs, PAIR, n_p), BF),
        jax.ShapeDtypeStruct((n_layers, batch, d_b, seq), F32),
        jax.ShapeDtypeStruct((n_layers, batch, d_b, seq), F32),
        jax.ShapeDtypeStruct((n_layers, batch, N_HEADS, seq), F32),
        jax.ShapeDtypeStruct((batch, HIST, d_a), F32),
        jax.ShapeDtypeStruct((TM, d_b), F32),
        jax.ShapeDtypeStruct((TM, d_b), F32),
        jax.ShapeDtypeStruct((TM, d_b), F32),
        jax.ShapeDtypeStruct((TM, d_a), F32),
        jax.ShapeDtypeStruct((TM, LANES), F32),
    )
    out_specs = (
        pl.BlockSpec((TM, d_a), p_row),
        pl.BlockSpec((n_pairs, PAIR, TM), p_col3),
        pl.BlockSpec((N_HEADS, TM, LANES), lambda i: (0, jnp.minimum(i, last_p), 0)),
        pl.BlockSpec((n_pairs, PAIR, TM), p_col3),
        pl.BlockSpec((None, None, d_b, TM), layer_col),
        pl.BlockSpec((None, None, d_b, TM), layer_col),
        pl.BlockSpec((None, None, N_HEADS, TM), layer_col),
        pl.BlockSpec((None, HIST, d_a),
                     lambda i: (jnp.minimum(i // tiles_per_seq, batch - 1), 0, 0)),
        pl.BlockSpec((TM, d_b), s_row),
        pl.BlockSpec((TM, d_b), s_row),
        pl.BlockSpec((TM, d_b), s_row),
        pl.BlockSpec((TM, d_a), s_row),
        pl.BlockSpec((TM, LANES), s_row),
    )
    in_specs = [
        pl.BlockSpec((TM, d), p_row),
        _resident(xs.shape),
        _resident(g.shape), _layer_slab(w_all, layer), _resident(bfg.shape),
        _resident(qg.shape), _resident(kg.shape), _resident(bd.shape),
        _resident(cw.shape), _resident(cb.shape), _resident(lg.shape), _resident(lb.shape),
        _resident(sel.shape),
    ] + [pl.BlockSpec(memory_space=pl.ANY)] * len(stacked)
    first_stacked_out = 4
    return pl.pallas_call(
        kernel,
        grid=(n_ptiles + 1,),
        in_specs=in_specs,
        out_specs=out_specs,
        out_shape=out_shape,
        input_output_aliases={len(in_specs) - len(stacked) + j: first_stacked_out + j
                              for j in range(len(stacked))},
        scratch_shapes=[pltpu.VMEM((HIST_PAD + TM, d_a), F32),
                        pltpu.VMEM((1, LANES), F32),
                        pltpu.VMEM((2, HIST_PAD + TM, d_a), F32)],
        compiler_params=pltpu.CompilerParams(
            dimension_semantics=("arbitrary",), vmem_limit_bytes=VMEM_LIMIT),
        name="even_in",
    )(xp, xs, g, w_all, bfg, qg, kg, bd, cw, cb, lg, lb, sel, *stacked)


def _sample_conv_kernel(glu_ref, hist_ref, cw_ref, cb_ref, lg_ref, lb_ref, a_ref,
                        *, dec_batch, dec_seq):
    d_a = cw_ref.shape[1]

    def full(j):
        if j < HIST:
            return hist_ref[j]
        return glu_ref[(j - HIST) * dec_batch:(j - HIST + 1) * dec_batch, :]

    for t in range(dec_seq):
        acc = jnp.zeros((dec_batch, d_a), F32)
        for kk in range(CONV_WIDTH):
            acc = acc + cw_ref[kk:kk + 1, :] * full(t + kk)
        y = _ln(acc + cb_ref[...], lg_ref[...], lb_ref[...])
        a_ref[t * dec_batch:(t + 1) * dec_batch, :] = (y * jax.nn.sigmoid(y)).astype(BF)


def _sample_conv(glu_s, hist2d, cw, cb, lg, lb, *, dec_batch, dec_seq):
    kernel = functools.partial(_sample_conv_kernel, dec_batch=dec_batch, dec_seq=dec_seq)
    return pl.pallas_call(
        kernel,
        out_shape=jax.ShapeDtypeStruct(glu_s.shape, BF),
        compiler_params=pltpu.CompilerParams(vmem_limit_bytes=VMEM_LIMIT),
        name="sample_conv",
    )(glu_s, hist2d, cw, cb, lg, lb)


def _flash_kernel(qt_ref, ka0_ref, ka1_ref, vt_ref, o_ref, s_ref, *, tq):
    qi = pl.program_id(2)
    q0 = pl.multiple_of(qi * tq, tq)
    q2 = qt_ref[...].astype(F32)
    row = lax.broadcasted_iota(jnp.int32, (PAIR, 1), 0)
    ones_rows = jnp.where(jnp.logical_and(row >= HEAD_DIM, row < HEAD_DIM + 3), 1.0, 0.0)
    swapped = jnp.concatenate([q2[HEAD_DIM:], q2[:HEAD_DIM]], axis=0)
    qa = tuple(jnp.where(row < HEAD_DIM, qq, ones_rows).astype(BF) for qq in (q2, swapped))
    ka = (ka0_ref, ka1_ref)

    def scores(ki, slot):
        start = pl.multiple_of(ki * tq, tq)
        for hh in range(2):
            s_ref[slot, hh] = _dot(ka[hh][pl.ds(start, tq), :], qa[hh])

    def tile(ki, slot, carry, masked):
        start = pl.multiple_of(ki * tq, tq)
        m, l, acc = carry
        new_m, new_l, new_acc = [], [], []
        for hh in range(2):
            s_t = s_ref[slot, hh]
            if masked:
                r = lax.broadcasted_iota(jnp.int32, (tq, tq), 0)
                c = lax.broadcasted_iota(jnp.int32, (tq, tq), 1)
                s_t = jnp.where(r <= c, s_t, NEG)
            m_new = jnp.maximum(m[hh], jnp.max(s_t, axis=0, keepdims=True))
            alpha = jnp.exp(m[hh] - m_new)
            p = jnp.exp(s_t - m_new)
            new_l.append(alpha * l[hh] + jnp.sum(p, axis=0, keepdims=True))
            new_m.append(m_new)
            v_t = vt_ref[hh * HEAD_DIM:(hh + 1) * HEAD_DIM, pl.ds(start, tq)]
            new_acc.append(alpha * acc[hh] + _dot(v_t, p.astype(BF)))
        return tuple(new_m), tuple(new_l), tuple(new_acc)

    neg = jnp.full((1, tq), NEG, F32)
    zrow = jnp.zeros((1, tq), F32)
    zacc = jnp.zeros((HEAD_DIM, tq), F32)
    init = ((neg, neg), (zrow, zrow), (zacc, zacc))

    def finish(carry):
        _, l, acc = carry
        out_t = jnp.concatenate([acc[0] / l[0], acc[1] / l[1]], axis=0)
        o_ref[...] = out_t.T.astype(BF)

    def pair(j, carry):
        scores(2 * j + 1, 1)
        carry = tile(2 * j, 0, carry, False)
        scores(2 * j + 2, 0)
        return tile(2 * j + 1, 1, carry, False)

    scores(0, 0)
    carry = lax.fori_loop(0, qi // 2, pair, init)

    @pl.when(qi % 2 == 0)
    def _():
        finish(tile(qi, 0, carry, True))

    @pl.when(qi % 2 == 1)
    def _():
        scores(qi, 1)
        finish(tile(qi, 1, tile(qi - 1, 0, carry, False), True))


def _flash(q_t, k_aug, v_t, *, batch, seq):
    n_pairs = q_t.shape[0]
    tq = TM
    nq = seq // tq
    kernel = functools.partial(_flash_kernel, tq=tq)
    t_spec = pl.BlockSpec((None, PAIR, seq), lambda b, hp, qi: (hp, 0, b))
    return pl.pallas_call(
        kernel,
        grid=(batch, n_pairs, nq),
        in_specs=[
            pl.BlockSpec((None, PAIR, tq), lambda b, hp, qi: (hp, 0, b * nq + qi)),
            pl.BlockSpec((None, seq, LANES), lambda b, hp, qi: (2 * hp, b, 0)),
            pl.BlockSpec((None, seq, LANES), lambda b, hp, qi: (2 * hp + 1, b, 0)),
            t_spec,
        ],
        out_specs=pl.BlockSpec((tq, PAIR), lambda b, hp, qi: (b * nq + qi, hp)),
        out_shape=jax.ShapeDtypeStruct((batch * seq, n_pairs * PAIR), BF),
        scratch_shapes=[pltpu.VMEM((2, 2, tq, tq), F32)],
        compiler_params=pltpu.CompilerParams(
            dimension_semantics=("arbitrary", "arbitrary", "arbitrary"),
            vmem_limit_bytes=VMEM_LIMIT),
        name="flash_prompt",
    )(q_t, k_aug, k_aug, v_t)


def _decode_kernel(pt_ref, q_ref, kn_ref, vn_ref, lfn_ref, *rest, n_pages, dec_seq, page):
    del pt_ref
    k_refs = rest[:n_pages]
    v_refs = rest[n_pages:2 * n_pages]
    lf_refs = rest[2 * n_pages:3 * n_pages]
    o_ref = rest[3 * n_pages]
    kx_ref, vx_ref = rest[3 * n_pages + 1:]
    d_b = N_HEADS * HEAD_DIM
    rows = dec_seq * N_HEADS

    @pl.when(pl.program_id(0) == 0)
    def _():
        kx_ref[...] = jnp.zeros_like(kx_ref)
        vx_ref[...] = jnp.zeros_like(vx_ref)

    kx_ref[0:dec_seq, :] = kn_ref[:, 0, :]
    vx_ref[0:dec_seq, :] = vn_ref[:, 0, :]

    lane_h = lax.broadcasted_iota(jnp.int32, (N_HEADS, d_b), 1) // HEAD_DIM
    row_h = lax.broadcasted_iota(jnp.int32, (N_HEADS, d_b), 0)
    head_mask = lane_h == row_h
    q4 = q_ref[:, 0, :].astype(F32)
    q32 = jnp.where(head_mask[None], q4[:, None, :], 0.0).reshape(rows, d_b).astype(BF)

    lf = jnp.concatenate([r[...] for r in lf_refs], axis=0)
    lane = lax.broadcasted_iota(jnp.int32, lf.shape, 1)
    sfx = lf
    d = 1
    while d < page:
        sfx = sfx + jnp.where(lane < page - d, pltpu.roll(sfx, page - d, axis=1), 0.0)
        d *= 2
    excl = sfx - lf
    tot = sfx[:, 0:1]
    bias = [None] * n_pages
    run = jnp.zeros((N_HEADS, 1), F32)
    for p in range(n_pages - 1, -1, -1):
        sl = slice(p * N_HEADS, (p + 1) * N_HEADS)
        bias[p] = excl[sl] + run
        run = run + tot[sl]

    s_pages = []
    for p in range(n_pages):
        k_t = k_refs[p][...].reshape(d_b, page).astype(BF)
        s = _dot(q32, k_t).reshape(dec_seq, N_HEADS, page)
        s_pages.append(s + bias[p][None])

    lfn = lfn_ref[...]
    lane8 = lax.broadcasted_iota(jnp.int32, lfn.shape, 1)
    cs = lfn
    d = 1
    while d < dec_seq:
        cs = cs + jnp.where(lane8 >= d, pltpu.roll(cs, d, axis=1), 0.0)
        d *= 2
    s_new = _dot_nt(q32, kx_ref[...].astype(BF)).reshape(dec_seq, N_HEADS, page) - cs[None]
    step = lax.broadcasted_iota(jnp.int32, (dec_seq, N_HEADS, page), 0)
    key = lax.broadcasted_iota(jnp.int32, (dec_seq, N_HEADS, page), 2)
    s_new = jnp.where(key <= step, s_new, NEG)

    mx = s_new
    for s in s_pages:
        mx = jnp.maximum(mx, s)
    m = jnp.max(mx, axis=2, keepdims=True)
    p_new = jnp.exp(s_new - m)
    psum = p_new
    o = _dot(p_new.reshape(rows, page).astype(BF), vx_ref[...].astype(BF))
    for p in range(n_pages):
        pp = jnp.exp(s_pages[p] - m)
        psum = psum + pp
        v_t = v_refs[p][...].reshape(d_b, page).astype(BF)
        o = o + _dot_nt(pp.reshape(rows, page).astype(BF), v_t)
    l = jnp.sum(psum, axis=2, keepdims=True)
    o = o.reshape(dec_seq, N_HEADS, d_b) / l
    o_ref[:, 0, :] = jnp.sum(jnp.where(head_mask[None], o, 0.0), axis=1).astype(BF)


def _decode(page_table_flat, q_s, k_s, v_s, lfn_t, cache_kt, cache_vt, cache_lf_t, *,
            layer, dec_batch, dec_seq):
    n_pages = page_table_flat.shape[0] // dec_batch
    page = cache_kt.shape[4]
    d_b = N_HEADS * HEAD_DIM
    tok_spec = pl.BlockSpec((dec_seq, None, 1, d_b), lambda b, pt: (0, b, 0, 0))

    def page_map(p, tail, b, pt):
        return (layer, pt[b * n_pages + p]) + (0,) * tail

    kv_specs = [pl.BlockSpec((None, None, N_HEADS, HEAD_DIM, page),
                             functools.partial(page_map, p, 3)) for p in range(n_pages)]
    lf_specs = [pl.BlockSpec((None, None, N_HEADS, page),
                             functools.partial(page_map, p, 2)) for p in range(n_pages)]
    kernel = functools.partial(_decode_kernel, n_pages=n_pages, dec_seq=dec_seq, page=page)
    grid_spec = pltpu.PrefetchScalarGridSpec(
        num_scalar_prefetch=1,
        grid=(dec_batch,),
        in_specs=[tok_spec, tok_spec, tok_spec,
                  pl.BlockSpec((None, N_HEADS, page), lambda b, pt: (b, 0, 0))]
        + kv_specs + kv_specs + lf_specs,
        out_specs=tok_spec,
        scratch_shapes=[pltpu.VMEM((page, d_b), F32), pltpu.VMEM((page, d_b), F32)],
    )
    return pl.pallas_call(
        kernel,
        grid_spec=grid_spec,
        out_shape=jax.ShapeDtypeStruct((dec_seq, dec_batch, 1, d_b), BF),
        compiler_params=pltpu.CompilerParams(
            dimension_semantics=("arbitrary",), vmem_limit_bytes=VMEM_LIMIT),
        name="decode_attn",
    )(page_table_flat, q_s, k_s, v_s, lfn_t,
      *([cache_kt] * n_pages), *([cache_vt] * n_pages), *([cache_lf_t] * n_pages))


def _even_out_kernel(xp_ref, xs_ref, ap_ref, as_ref, tp_ref, ts_ref, woa_ref, wob_ref,
                     g_ref, wu_ref, wd_ref, yp_ref, ys_ref, *, n_ptiles):
    i = pl.program_id(0)
    is_s = i >= n_ptiles
    x = jnp.where(is_s, xs_ref[...], xp_ref[...])
    a = jnp.where(is_s, as_ref[...], ap_ref[...])
    t = jnp.where(is_s, ts_ref[...], tp_ref[...])
    x = x + (_dot(a, woa_ref[...]) + _dot(t, wob_ref[...]))
    y = _ffn(x, g_ref, wu_ref, wd_ref)

    @pl.when(is_s)
    def _():
        ys_ref[...] = y

    @pl.when(jnp.logical_not(is_s))
    def _():
        yp_ref[...] = y


def _even_out(xp, xs, a_p, a_s, t_p, t_s, wo_all, g, wu_all, wd_all, *, e, layer):
    n_p, d = xp.shape
    d_a = a_p.shape[1]
    d_b = t_p.shape[1]
    n_ptiles = n_p // TM
    last_p = n_ptiles - 1
    p_row = lambda i: (jnp.minimum(i, last_p), 0)
    kernel = functools.partial(_even_out_kernel, n_ptiles=n_ptiles)
    assert d_a == d_b
    return pl.pallas_call(
        kernel,
        grid=(n_ptiles + 1,),
        in_specs=[
            pl.BlockSpec((TM, d), p_row), _resident(xs.shape),
            pl.BlockSpec((TM, d_a), p_row), _resident(a_s.shape),
            pl.BlockSpec((TM, d_b), p_row), _resident(t_s.shape),
            _layer_slab(wo_all, e, (d_a, d), (0, 0)), _layer_slab(wo_all, e, (d_b, d), (1, 0)),
            _resident(g.shape), _layer_slab(wu_all, layer), _layer_slab(wd_all, layer),
        ],
        out_specs=(pl.BlockSpec((TM, d), p_row), pl.BlockSpec((TM, d), lambda i: (0, 0))),
        out_shape=(jax.ShapeDtypeStruct(xp.shape, F32), jax.ShapeDtypeStruct(xs.shape, F32)),
        compiler_params=pltpu.CompilerParams(
            dimension_semantics=("arbitrary",), vmem_limit_bytes=VMEM_LIMIT),
        name="even_out",
    )(xp, xs, a_p, a_s, t_p, t_s, wo_all, wo_all, g, wu_all, wd_all)


def _odd_kernel(xp_ref, xs_ref, gm_ref, wi_ref, lg_ref, lb_ref, ws_ref, bs_ref, wsm_ref,
                wo_ref, gf_ref, wu_ref, wd_ref, yp_ref, ys_ref, vs_ref, mix_ref,
                *, n_ptiles, dec_batch, dec_seq):
    i = pl.program_id(0)
    is_s = i >= n_ptiles
    d_c = wo_ref.shape[0]
    n_groups = ws_ref.shape[0]
    c_hd = d_c // n_groups

    x = jnp.where(is_s, xs_ref[...], xp_ref[...])
    h = _rms(x, gm_ref[...]).astype(BF)
    u = jax.nn.gelu(_dot(h, wi_ref[:, :d_c]))
    v = _ln(jax.nn.gelu(_dot(h, wi_ref[:, d_c:])), lg_ref[...], lb_ref[...])
    v_b = v.astype(BF)

    r = lax.broadcasted_iota(jnp.int32, (CHUNK, CHUNK), 0)
    c = lax.broadcasted_iota(jnp.int32, (CHUNK, CHUNK), 1)
    for hg in range(n_groups):
        w = jnp.where(r >= c, ws_ref[hg], jnp.zeros((CHUNK, CHUNK), BF))
        cols = slice(hg * c_hd, (hg + 1) * c_hd)
        for cb in range(TM // CHUNK):
            rows = slice(cb * CHUNK, (cb + 1) * CHUNK)
            mix_ref[rows, cols] = _dot(w, v_b[rows, cols]) + bs_ref[:, cols]

    v_r = v_b.astype(F32)
    mix_s = []
    for t in range(dec_seq):
        acc = jnp.zeros((dec_batch, d_c), F32) + bs_ref[t:t + 1, :]
        for j in range(t + 1):
            wt = wsm_ref[t * dec_seq + j:t * dec_seq + j + 1, :].astype(BF).astype(F32)
            acc = acc + wt * v_r[j * dec_batch:(j + 1) * dec_batch, :]
        mix_s.append(acc)
    mix = jnp.where(is_s, jnp.concatenate(mix_s, axis=0), mix_ref[...])

    gte = (u * mix).astype(BF)
    x = x + _dot(gte, wo_ref[...])
    y = _ffn(x, gf_ref, wu_ref, wd_ref)

    @pl.when(is_s)
    def _():
        ys_ref[...] = y
        vs_ref[...] = v

    @pl.when(jnp.logical_not(is_s))
    def _():
        yp_ref[...] = y


def _odd(xp, xs, gm, wi_all, lg, lb, ws_all, bs_full, ws_small, wo_all, gf, wu_all, wd_all, *,
         o, layer, dec_batch, dec_seq):
    n_p, d = xp.shape
    d_c = wo_all.shape[1]
    n_ptiles = n_p // TM
    last_p = n_ptiles - 1
    p_row = lambda i: (jnp.minimum(i, last_p), 0)
    s_row = lambda i: (0, 0)
    kernel = functools.partial(_odd_kernel, n_ptiles=n_ptiles, dec_batch=dec_batch,
                               dec_seq=dec_seq)
    return pl.pallas_call(
        kernel,
        grid=(n_ptiles + 1,),
        in_specs=[
            pl.BlockSpec((TM, d), p_row), _resident(xs.shape),
            _resident(gm.shape), _layer_slab(wi_all, o), _resident(lg.shape), _resident(lb.shape),
            _layer_slab(ws_all, o), _resident(bs_full.shape), _resident(ws_small.shape),
            _layer_slab(wo_all, o), _resident(gf.shape),
            _layer_slab(wu_all, layer), _layer_slab(wd_all, layer),
        ],
        out_specs=(pl.BlockSpec((TM, d), p_row), pl.BlockSpec((TM, d), s_row),
                   pl.BlockSpec((TM, d_c), s_row)),
        out_shape=(jax.ShapeDtypeStruct(xp.shape, F32), jax.ShapeDtypeStruct(xs.shape, F32),
                   jax.ShapeDtypeStruct((TM, d_c), F32)),
        scratch_shapes=[pltpu.VMEM((TM, d_c), F32)],
        compiler_params=pltpu.CompilerParams(
            dimension_semantics=("arbitrary",), vmem_limit_bytes=VMEM_LIMIT),
        name="odd_layer",
    )(xp, xs, gm, wi_all, lg, lb, ws_all, bs_full, ws_small, wo_all, gf, wu_all, wd_all)


def kernel(x_prompt, x_sample, cache_k, cache_v, cache_logf, state_conv, page_table,
           norm_mix_g, norm_ffn_g, w_in_ab, b_forget, q_norm_g, k_norm_g,
           conv_w, conv_b, conv_ln_g, conv_ln_b, w_out_ab,
           w_in_c, sgu_ln_g, sgu_ln_b, w_spatial, b_spatial, w_out_c,
           w_ff_up, w_ff_down):
    batch, seq, d = x_prompt.shape
    dec_batch, dec_seq, _ = x_sample.shape
    depth = norm_mix_g.shape[0]
    n_ab, n_pool, page, n_heads, head_dim = cache_k.shape
    d_b = n_heads * head_dim
    d_a = conv_w.shape[2]
    n_groups = w_spatial.shape[1]
    d_c = w_out_c.shape[1]
    assert dec_batch * dec_seq == TM and seq % TM == 0 and page == LANES
    assert n_heads == N_HEADS and head_dim == HEAD_DIM and conv_w.shape[1] == CONV_WIDTH
    assert w_spatial.shape[2] == CHUNK and dec_seq <= SUBLANES

    xp = x_prompt.reshape(batch * seq, d)
    xs = jnp.swapaxes(x_sample, 0, 1).reshape(TM, d)

    row = lambda a: a.reshape(1, -1).astype(F32)
    ck_t = jnp.transpose(cache_k, (0, 1, 3, 4, 2))
    cv_t = jnp.transpose(cache_v, (0, 1, 3, 4, 2))
    clf_t = jnp.swapaxes(cache_logf, 2, 3)
    hist_t = jnp.swapaxes(state_conv, 1, 2)
    pt_flat = page_table.reshape(-1)
    head_of = jnp.arange(d_b) // head_dim
    bd = (head_of[:, None] == head_of[None, :]).astype(BF) * (1.0 / head_dim)
    src = jnp.arange(LANES)
    dst = (src % n_heads) * LANES + head_dim + src // n_heads
    sel = jnp.logical_and(src[:, None] < 3 * n_heads,
                          jnp.arange(n_heads * LANES)[None, :] == dst[:, None]).astype(BF)

    wu_all = w_ff_up.astype(BF)
    wd_all = w_ff_down.astype(BF)
    w_in_all = jnp.pad(w_in_ab, ((0, 0), (0, 0), (0, LANES - n_heads))).astype(BF)
    wo_ab_all = w_out_ab.astype(BF)
    wi_c_all = w_in_c.astype(BF)
    wo_c_all = w_out_c.astype(BF)
    ws_all = w_spatial.astype(BF)

    outs = {k: [] for k in ("cp", "ks", "vs", "fs", "cs", "chs")}
    stacked = (jnp.zeros((n_ab, batch, d_b, seq), F32), jnp.zeros((n_ab, batch, d_b, seq), F32),
               jnp.zeros((n_ab, batch, n_heads, seq), F32))
    for layer in range(depth):
        g_mix = row(norm_mix_g[layer])
        g_ffn = row(norm_ffn_g[layer])
        if layer % 2 == 0:
            e = layer // 2
            bfg = jnp.pad(b_forget[e], (0, LANES - n_heads)).reshape(1, LANES)
            qg = row(jnp.tile(q_norm_g[e], n_heads))
            kg = row(jnp.tile(k_norm_g[e], n_heads))
            cw, cb = conv_w[e], row(conv_b[e])
            lg, lb = row(conv_ln_g[e]), row(conv_ln_b[e])
            (a_p, q_t, k_aug, v_t, kt_all, vt_all, lft_all, hist_p,
             q_s, k_s, v_s, glu_s, lf_s) = _even_in(
                xp, xs, g_mix, w_in_all, bfg, qg, kg, bd, cw, cb, lg, lb, sel, stacked,
                layer=e, n_layers=n_ab, batch=batch, seq=seq)
            stacked = (kt_all, vt_all, lft_all)

            a_s = _sample_conv(glu_s, hist_t[e], cw, cb, lg, lb,
                               dec_batch=dec_batch, dec_seq=dec_seq)

            t_p = _flash(q_t, k_aug, v_t, batch=batch, seq=seq)

            lf_s4 = lf_s[:, :n_heads].reshape(dec_seq, dec_batch, n_heads)
            lfn_t = jnp.pad(jnp.transpose(lf_s4, (1, 2, 0)),
                            ((0, 0), (0, 0), (0, page - dec_seq)))
            tok = lambda a: a.reshape(dec_seq, dec_batch, 1, d_b)
            t_s = _decode(pt_flat, tok(q_s), tok(k_s), tok(v_s), lfn_t, ck_t, cv_t, clf_t,
                          layer=e, dec_batch=dec_batch, dec_seq=dec_seq).reshape(TM, d_b)

            xp, xs = _even_out(xp, xs, a_p, a_s, t_p, t_s, wo_ab_all, g_ffn, wu_all, wd_all,
                               e=e, layer=layer)

            sample_major = lambda a: jnp.swapaxes(a.reshape(dec_seq, dec_batch, -1), 0, 1)
            outs["cp"].append(hist_p)
            outs["ks"].append(sample_major(k_s).reshape(dec_batch, dec_seq, n_heads, head_dim))
            outs["vs"].append(sample_major(v_s).reshape(dec_batch, dec_seq, n_heads, head_dim))
            outs["fs"].append(sample_major(lf_s[:, :n_heads]))
            outs["cs"].append(jnp.swapaxes(jnp.concatenate(
                [hist_t[e][dec_seq:], glu_s.reshape(dec_seq, dec_batch, d_a)], axis=0), 0, 1))
        else:
            o = layer // 2
            bs_full = jnp.repeat(b_spatial[o].T, d_c // n_groups, axis=1)
            ws_small = jnp.repeat(
                jnp.transpose(w_spatial[o][:, :dec_seq, :dec_seq], (1, 2, 0)),
                d_c // n_groups, axis=2).reshape(dec_seq * dec_seq, d_c)
            xp, xs, v_rows = _odd(
                xp, xs, g_mix, wi_c_all, row(sgu_ln_g[o]), row(sgu_ln_b[o]),
                ws_all, bs_full, ws_small, wo_c_all, g_ffn, wu_all, wd_all,
                o=o, layer=layer, dec_batch=dec_batch, dec_seq=dec_seq)
            outs["chs"].append(jnp.swapaxes(v_rows.reshape(dec_seq, dec_batch, d_c), 0, 1))

    y_prompt = xp.reshape(batch, seq, d)
    y_sample = jnp.swapaxes(xs.reshape(dec_seq, dec_batch, d), 0, 1)
    kt_all, vt_all, lft_all = stacked
    to_seq_major = lambda a: jnp.transpose(
        a.reshape(n_ab, batch, n_heads, head_dim, seq), (0, 1, 4, 2, 3))
    return (y_prompt, y_sample,
            to_seq_major(kt_all), to_seq_major(vt_all), jnp.swapaxes(lft_all, 2, 3),
            jnp.stack(outs["ks"]), jnp.stack(outs["vs"]), jnp.stack(outs["fs"]),
            jnp.stack(outs["cp"]), jnp.stack(outs["cs"]), jnp.stack(outs["chs"]))
```

```python
import functools

import jax
import jax.numpy as jnp
from jax import lax
from jax.experimental import pallas as pl
from jax.experimental.pallas import tpu as pltpu

BF = jnp.bfloat16
F32 = jnp.float32

TM = 512
LANES = 128
SUBLANES = 8
N_HEADS = 8
HEAD_DIM = 64
PAIR = 2 * HEAD_DIM
CONV_WIDTH = 31
HIST = CONV_WIDTH - 1
HIST_PAD = 32
CHUNK = 128
RMS_EPS = 1e-6
LN_EPS = 1e-5
NEG = -1e30
LOG2E = 1.4426950408889634
VMEM_LIMIT = 56 * 1024 * 1024


def _dot(a, b):
    return jnp.dot(a, b, preferred_element_type=F32)


def _dot_nt(a, b):
    return lax.dot_general(a, b, (((1,), (1,)), ((), ())), preferred_element_type=F32)


def _rms(x, g):
    ms = jnp.mean(x * x, axis=-1, keepdims=True)
    return x * lax.rsqrt(ms + RMS_EPS) * g


def _ln(x, g, b):
    mu = jnp.mean(x, axis=-1, keepdims=True)
    xc = x - mu
    var = jnp.mean(xc * xc, axis=-1, keepdims=True)
    return xc * lax.rsqrt(var + LN_EPS) * g + b


def _log_sigmoid(z):
    return jnp.minimum(z, 0.0) - jnp.log1p(jnp.exp(-jnp.abs(z)))


def _split3(x):
    hi = x.astype(BF)
    r1 = x - hi.astype(F32)
    mid = r1.astype(BF)
    lo = (r1 - mid.astype(F32)).astype(BF)
    return hi, mid, lo


def _ffn(x, g_ref, wu_ref, wd_ref, chunk=1024):
    hn = _rms(x, g_ref[...]).astype(BF)
    acc = None
    for c in range(wu_ref.shape[1] // chunk):
        a = _dot(hn, wu_ref[:, c * chunk:(c + 1) * chunk])
        a = jnp.maximum(a, 0.0)
        part = _dot((a * a).astype(BF), wd_ref[c * chunk:(c + 1) * chunk, :])
        acc = part if acc is None else acc + part
    return x + acc


def _resident(shape):
    nd = len(shape)
    return pl.BlockSpec(shape, lambda *_: (0,) * nd, pipeline_mode=pl.Buffered(1))


def _layer_slab(stacked, layer, block=None, index=None):
    block = stacked.shape[1:] if block is None else block
    index = (0,) * len(block) if index is None else index
    return pl.BlockSpec((None,) + tuple(block), lambda *_: (layer,) + tuple(index),
                        pipeline_mode=pl.Buffered(1))


def _even_in_kernel(xp_ref, xs_ref, g_ref, w_ref, bfg_ref, qg_ref, kg_ref, bd_ref,
                    cw_ref, cb_ref, lg_ref, lb_ref, sel_ref, *rest,
                    n_ptiles, tiles_per_seq, n_aliased):
    (ap_ref, qt_ref, kaug_ref, vt_ref, ktp_ref, vtp_ref, lft_ref,
     histp_ref, qs_ref, ks_ref, vs_ref, glus_ref, lfs_ref,
     gbuf_ref, fcarry_ref, shift_ref) = rest[n_aliased:]
    i = pl.program_id(0)
    is_s = i >= n_ptiles
    is_p = jnp.logical_not(is_s)
    seq_first = jnp.logical_and(is_p, i % tiles_per_seq == 0)
    seq_last = jnp.logical_and(is_p, i % tiles_per_seq == tiles_per_seq - 1)
    d_a = cw_ref.shape[1]
    d_b = N_HEADS * HEAD_DIM

    def project(x):
        h = _rms(x, g_ref[...]).astype(BF)
        a_in = _dot(h, w_ref[:, 0:2 * d_a])
        glu = a_in[:, :d_a] * jax.nn.sigmoid(a_in[:, d_a:])
        c0 = 2 * d_a
        q = _dot(h, w_ref[:, c0:c0 + d_b])
        k = _dot(h, w_ref[:, c0 + d_b:c0 + 2 * d_b])
        v = _dot(h, w_ref[:, c0 + 2 * d_b:c0 + 3 * d_b])
        fg = _dot(h, w_ref[:, c0 + 3 * d_b:c0 + 3 * d_b + LANES])
        bd = bd_ref[...]
        qn = q * lax.rsqrt(_dot((q * q).astype(BF), bd) + RMS_EPS) * qg_ref[...]
        kn = k * lax.rsqrt(_dot((k * k).astype(BF), bd) + RMS_EPS) * kg_ref[...]
        logf = _log_sigmoid(fg + bfg_ref[...])
        q_sc = qn * (HEAD_DIM ** -0.5)
        return glu, q_sc, kn, v, logf

    @pl.when(is_s)
    def _():
        glu, q_sc, kn, v, logf = project(xs_ref[...])
        qs_ref[...] = q_sc
        ks_ref[...] = kn
        vs_ref[...] = v
        glus_ref[...] = glu
        lfs_ref[...] = logf

    @pl.when(seq_first)
    def _():
        gbuf_ref[0:HIST_PAD, :] = jnp.zeros((HIST_PAD, d_a), F32)
        fcarry_ref[...] = jnp.zeros_like(fcarry_ref)

    @pl.when(is_p)
    def _():
        glu, q_sc, kn, v, logf = project(xp_ref[...])
        q_t = q_sc.T
        k_t = kn.T
        v_t = v.T
        ktp_ref[...] = k_t
        vtp_ref[...] = v_t
        lft_ref[...] = logf.T[0:N_HEADS, :]
        for hp in range(d_b // PAIR):
            sl = slice(hp * PAIR, (hp + 1) * PAIR)
            qt_ref[hp] = q_t[sl, :].astype(BF)
            vt_ref[hp] = v_t[sl, :].astype(BF)

        r = lax.broadcasted_iota(jnp.int32, (CHUNK, CHUNK), 0)
        c = lax.broadcasted_iota(jnp.int32, (CHUNK, CHUNK), 1)
        tril = (r >= c).astype(F32).astype(BF)
        hi, mid, lo = _split3(logf)
        carry = fcarry_ref[...]
        f_chunks = []
        for cb in range(TM // CHUNK):
            sl = slice(cb * CHUNK, (cb + 1) * CHUNK)
            fc = _dot(tril, hi[sl]) + _dot(tril, mid[sl]) + _dot(tril, lo[sl]) + carry
            f_chunks.append(fc)
            carry = fc[CHUNK - 1:CHUNK, :]
        fcarry_ref[...] = carry

        nf_hi, nf_mid, nf_lo = _split3(jnp.concatenate(f_chunks, axis=0) * (-LOG2E))
        lane = lax.broadcasted_iota(jnp.int32, (1, LANES), 1)
        parts = jnp.where(
            lane < N_HEADS, nf_hi.astype(F32),
            jnp.where(lane < 2 * N_HEADS, pltpu.roll(nf_mid.astype(F32), N_HEADS, axis=1),
                      pltpu.roll(nf_lo.astype(F32), 2 * N_HEADS, axis=1))).astype(BF)
        aug = _dot(parts, sel_ref[...])
        for hd in range(N_HEADS):
            pair = kn[:, (hd // 2) * PAIR:(hd // 2 + 1) * PAIR] * LOG2E
            if hd % 2:
                pair = pltpu.roll(pair, HEAD_DIM, axis=1)
            kaug_ref[hd] = jnp.where(lane < HEAD_DIM, pair,
                                     aug[:, hd * LANES:(hd + 1) * LANES]).astype(BF)

        gbuf_ref[HIST_PAD:HIST_PAD + TM, :] = glu
        acc = jnp.zeros((TM, d_a), F32)
        for b in range(SUBLANES):
            a_max = (CONV_WIDTH - 1 - b) // SUBLANES
            rows = TM + SUBLANES * a_max
            shift_ref[b % 2, 0:rows, :] = gbuf_ref[pl.ds(HIST_PAD - HIST + b, rows), :]
            for a in range(a_max + 1):
                t = SUBLANES * a + b
                acc = acc + cw_ref[t:t + 1, :] * shift_ref[b % 2, SUBLANES * a:SUBLANES * a + TM, :]
        y = _ln(acc + cb_ref[...], lg_ref[...], lb_ref[...])
        ap_ref[...] = (y * jax.nn.sigmoid(y)).astype(BF)
        gbuf_ref[0:HIST_PAD, :] = gbuf_ref[TM:TM + HIST_PAD, :]

    @pl.when(seq_last)
    def _():
        histp_ref[...] = gbuf_ref[pl.ds(HIST_PAD - HIST, HIST), :]


def _even_in(xp, xs, g, w_all, bfg, qg, kg, bd, cw, cb, lg, lb, sel, stacked, *,
             layer, n_layers, batch, seq):
    n_p, d = xp.shape
    d_a = cw.shape[1]
    d_b = N_HEADS * HEAD_DIM
    n_pairs = d_b // PAIR
    n_ptiles = n_p // TM
    tiles_per_seq = seq // TM
    last_p = n_ptiles - 1
    p_row = lambda i: (jnp.minimum(i, last_p), 0)
    p_col3 = lambda i: (0, 0, jnp.minimum(i, last_p))
    s_row = lambda i: (0, 0)
    seq_col = lambda i: (jnp.minimum(i, last_p) // tiles_per_seq, 0,
                         jnp.minimum(i, last_p) % tiles_per_seq)
    stacked = tuple(stacked)
    layer_col = lambda i: (layer,) + seq_col(i)
    kernel = functools.partial(_even_in_kernel, n_ptiles=n_ptiles, tiles_per_seq=tiles_per_seq,
                               n_aliased=len(stacked))
    out_shape = (
        jax.ShapeDtypeStruct((n_p, d_a), BF),
        jax.ShapeDtypeStruct((n_pairs, PAIR, n_p), BF),
        jax.ShapeDtypeStruct((N_HEADS, n_p, LANES), BF),
        jax.ShapeDtypeStruct((n_pairs, PAIR, n_p), BF),
        jax.ShapeDtypeStruct((n_layers, batch, d_b, seq), F32),
        jax.ShapeDtypeStruct((n_layers, batch, d_b, seq), F32),
        jax.ShapeDtypeStruct((n_layers, batch, N_HEADS, seq), F32),
        jax.ShapeDtypeStruct((batch, HIST, d_a), F32),
        jax.ShapeDtypeStruct((TM, d_b), F32),
        jax.ShapeDtypeStruct((TM, d_b), F32),
        jax.ShapeDtypeStruct((TM, d_b), F32),
        jax.ShapeDtypeStruct((TM, d_a), F32),
        jax.ShapeDtypeStruct((TM, LANES), F32),
    )
    out_specs = (
        pl.BlockSpec((TM, d_a), p_row),
        pl.BlockSpec((n_pairs, PAIR, TM), p_col3),
        pl.BlockSpec((N_HEADS, TM, LANES), lambda i: (0, jnp.minimum(i, last_p), 0)),
        pl.BlockSpec((n_pairs, PAIR, TM), p_col3),
        pl.BlockSpec((None, None, d_b, TM), layer_col),
        pl.BlockSpec((None, None, d_b, TM), layer_col),
        pl.BlockSpec((None, None, N_HEADS, TM), layer_col),
        pl.BlockSpec((None, HIST, d_a),
                     lambda i: (jnp.minimum(i // tiles_per_seq, batch - 1), 0, 0)),
        pl.BlockSpec((TM, d_b), s_row),
        pl.BlockSpec((TM, d_b), s_row),
        pl.BlockSpec((TM, d_b), s_row),
        pl.BlockSpec((TM, d_a), s_row),
        pl.BlockSpec((TM, LANES), s_row),
    )
    in_specs = [
        pl.BlockSpec((TM, d), p_row),
        _resident(xs.shape),
        _resident(g.shape), _layer_slab(w_all, layer), _resident(bfg.shape),
        _resident(qg.shape), _resident(kg.shape), _resident(bd.shape),
        _resident(cw.shape), _resident(cb.shape), _resident(lg.shape), _resident(lb.shape),
        _resident(sel.shape),
    ] + [pl.BlockSpec(memory_space=pl.ANY)] * len(stacked)
    first_stacked_out = 4
    return pl.pallas_call(
        kernel,
        grid=(n_ptiles + 1,),
        in_specs=in_specs,
        out_specs=out_specs,
        out_shape=out_shape,
        input_output_aliases={len(in_specs) - len(stacked) + j: first_stacked_out + j
                              for j in range(len(stacked))},
        scratch_shapes=[pltpu.VMEM((HIST_PAD + TM, d_a), F32),
                        pltpu.VMEM((1, LANES), F32),
                        pltpu.VMEM((2, HIST_PAD + TM, d_a), F32)],
        compiler_params=pltpu.CompilerParams(
            dimension_semantics=("arbitrary",), vmem_limit_bytes=VMEM_LIMIT),
        name="even_in",
    )(xp, xs, g, w_all, bfg, qg, kg, bd, cw, cb, lg, lb, sel, *stacked)


def _sample_conv_kernel(glu_ref, hist_ref, cw_ref, cb_ref, lg_ref, lb_ref, a_ref,
                        *, dec_batch, dec_seq):
    d_a = cw_ref.shape[1]

    def full(j):
        if j < HIST:
            return hist_ref[j]
        return glu_ref[(j - HIST) * dec_batch:(j - HIST + 1) * dec_batch, :]

    for t in range(dec_seq):
        acc = jnp.zeros((dec_batch, d_a), F32)
        for kk in range(CONV_WIDTH):
            acc = acc + cw_ref[kk:kk + 1, :] * full(t + kk)
        y = _ln(acc + cb_ref[...], lg_ref[...], lb_ref[...])
        a_ref[t * dec_batch:(t + 1) * dec_batch, :] = (y * jax.nn.sigmoid(y)).astype(BF)


def _sample_conv(glu_s, hist2d, cw, cb, lg, lb, *, dec_batch, dec_seq):
    kernel = functools.partial(_sample_conv_kernel, dec_batch=dec_batch, dec_seq=dec_seq)
    return pl.pallas_call(
        kernel,
        out_shape=jax.ShapeDtypeStruct(glu_s.shape, BF),
        compiler_params=pltpu.CompilerParams(vmem_limit_bytes=VMEM_LIMIT),
        name="sample_conv",
    )(glu_s, hist2d, cw, cb, lg, lb)


def _flash_kernel(qt_ref, ka0_ref, ka1_ref, vt_ref, o_ref, s_ref, *, tq):
    qi = pl.program_id(2)
    q0 = pl.multiple_of(qi * tq, tq)
    q2 = qt_ref[...].astype(F32)
    row = lax.broadcasted_iota(jnp.int32, (PAIR, 1), 0)
    ones_rows = jnp.where(jnp.logical_and(row >= HEAD_DIM, row < HEAD_DIM + 3), 1.0, 0.0)
    swapped = jnp.concatenate([q2[HEAD_DIM:], q2[:HEAD_DIM]], axis=0)
    qa = tuple(jnp.where(row < HEAD_DIM, qq, ones_rows).astype(BF) for qq in (q2, swapped))
    ka = (ka0_ref, ka1_ref)

    def scores(ki, slot):
        start = pl.multiple_of(ki * tq, tq)
        for hh in range(2):
            s_ref[slot, hh] = _dot(ka[hh][pl.ds(start, tq), :], qa[hh])

    def tile(ki, slot, carry, masked):
        start = pl.multiple_of(ki * tq, tq)
        m, l, acc = carry
        new_m, new_l, new_acc = [], [], []
        for hh in range(2):
            s_t = s_ref[slot, hh]
            if masked:
                r = lax.broadcasted_iota(jnp.int32, (tq, tq), 0)
                c = lax.broadcasted_iota(jnp.int32, (tq, tq), 1)
                s_t = jnp.where(r <= c, s_t, NEG)
            m_new = jnp.maximum(m[hh], jnp.max(s_t, axis=0, keepdims=True))
            alpha = jnp.exp2(m[hh] - m_new)
            p = jnp.exp2(s_t - m_new)
            new_l.append(alpha * l[hh] + jnp.sum(p, axis=0, keepdims=True))
            new_m.append(m_new)
            v_t = vt_ref[hh * HEAD_DIM:(hh + 1) * HEAD_DIM, pl.ds(start, tq)]
            new_acc.append(alpha * acc[hh] + _dot(v_t, p.astype(BF)))
        return tuple(new_m), tuple(new_l), tuple(new_acc)

    neg = jnp.full((1, tq), NEG, F32)
    zrow = jnp.zeros((1, tq), F32)
    zacc = jnp.zeros((HEAD_DIM, tq), F32)
    init = ((neg, neg), (zrow, zrow), (zacc, zacc))

    def finish(carry):
        _, l, acc = carry
        out_t = jnp.concatenate([acc[0] / l[0], acc[1] / l[1]], axis=0)
        o_ref[...] = out_t.T.astype(BF)

    def pair(j, carry):
        scores(2 * j + 1, 1)
        carry = tile(2 * j, 0, carry, False)
        scores(2 * j + 2, 0)
        return tile(2 * j + 1, 1, carry, False)

    scores(0, 0)
    carry = lax.fori_loop(0, qi // 2, pair, init)

    @pl.when(qi % 2 == 0)
    def _():
        finish(tile(qi, 0, carry, True))

    @pl.when(qi % 2 == 1)
    def _():
        scores(qi, 1)
        finish(tile(qi, 1, tile(qi - 1, 0, carry, False), True))


def _flash(q_t, k_aug, v_t, *, batch, seq):
    n_pairs = q_t.shape[0]
    tq = TM
    nq = seq // tq
    kernel = functools.partial(_flash_kernel, tq=tq)
    t_spec = pl.BlockSpec((None, PAIR, seq), lambda b, hp, qi: (hp, 0, b))
    return pl.pallas_call(
        kernel,
        grid=(batch, n_pairs, nq),
        in_specs=[
            pl.BlockSpec((None, PAIR, tq), lambda b, hp, qi: (hp, 0, b * nq + qi)),
            pl.BlockSpec((None, seq, LANES), lambda b, hp, qi: (2 * hp, b, 0)),
            pl.BlockSpec((None, seq, LANES), lambda b, hp, qi: (2 * hp + 1, b, 0)),
            t_spec,
        ],
        out_specs=pl.BlockSpec((tq, PAIR), lambda b, hp, qi: (b * nq + qi, hp)),
        out_shape=jax.ShapeDtypeStruct((batch * seq, n_pairs * PAIR), BF),
        scratch_shapes=[pltpu.VMEM((2, 2, tq, tq), F32)],
        compiler_params=pltpu.CompilerParams(
            dimension_semantics=("arbitrary", "arbitrary", "arbitrary"),
            vmem_limit_bytes=VMEM_LIMIT),
        name="flash_prompt",
    )(q_t, k_aug, k_aug, v_t)


def _decode_kernel(pt_ref, q_ref, kn_ref, vn_ref, lfn_ref, *rest, n_pages, dec_seq, page):
    del pt_ref
    k_refs = rest[:n_pages]
    v_refs = rest[n_pages:2 * n_pages]
    lf_refs = rest[2 * n_pages:3 * n_pages]
    o_ref = rest[3 * n_pages]
    kx_ref, vx_ref = rest[3 * n_pages + 1:]
    d_b = N_HEADS * HEAD_DIM
    rows = dec_seq * N_HEADS

    @pl.when(pl.program_id(0) == 0)
    def _():
        kx_ref[...] = jnp.zeros_like(kx_ref)
        vx_ref[...] = jnp.zeros_like(vx_ref)

    kx_ref[0:dec_seq, :] = kn_ref[:, 0, :]
    vx_ref[0:dec_seq, :] = vn_ref[:, 0, :]

    lane_h = lax.broadcasted_iota(jnp.int32, (N_HEADS, d_b), 1) // HEAD_DIM
    row_h = lax.broadcasted_iota(jnp.int32, (N_HEADS, d_b), 0)
    head_mask = lane_h == row_h
    q4 = q_ref[:, 0, :].astype(F32)
    q32 = jnp.where(head_mask[None], q4[:, None, :], 0.0).reshape(rows, d_b).astype(BF)

    lf = jnp.concatenate([r[...] for r in lf_refs], axis=0)
    lane = lax.broadcasted_iota(jnp.int32, lf.shape, 1)
    sfx = lf
    d = 1
    while d < page:
        sfx = sfx + jnp.where(lane < page - d, pltpu.roll(sfx, page - d, axis=1), 0.0)
        d *= 2
    excl = sfx - lf
    tot = sfx[:, 0:1]
    bias = [None] * n_pages
    run = jnp.zeros((N_HEADS, 1), F32)
    for p in range(n_pages - 1, -1, -1):
        sl = slice(p * N_HEADS, (p + 1) * N_HEADS)
        bias[p] = excl[sl] + run
        run = run + tot[sl]

    s_pages = []
    for p in range(n_pages):
        k_t = k_refs[p][...].reshape(d_b, page).astype(BF)
        s = _dot(q32, k_t).reshape(dec_seq, N_HEADS, page)
        s_pages.append(s + bias[p][None])

    lfn = lfn_ref[...]
    lane8 = lax.broadcasted_iota(jnp.int32, lfn.shape, 1)
    cs = lfn
    d = 1
    while d < dec_seq:
        cs = cs + jnp.where(lane8 >= d, pltpu.roll(cs, d, axis=1), 0.0)
        d *= 2
    s_new = _dot_nt(q32, kx_ref[...].astype(BF)).reshape(dec_seq, N_HEADS, page) - cs[None]
    step = lax.broadcasted_iota(jnp.int32, (dec_seq, N_HEADS, page), 0)
    key = lax.broadcasted_iota(jnp.int32, (dec_seq, N_HEADS, page), 2)
    s_new = jnp.where(key <= step, s_new, NEG)

    mx = s_new
    for s in s_pages:
        mx = jnp.maximum(mx, s)
    m = jnp.max(mx, axis=2, keepdims=True)
    p_new = jnp.exp(s_new - m)
    psum = p_new
    o = _dot(p_new.reshape(rows, page).astype(BF), vx_ref[...].astype(BF))
    for p in range(n_pages):
        pp = jnp.exp(s_pages[p] - m)
        psum = psum + pp
        v_t = v_refs[p][...].reshape(d_b, page).astype(BF)
        o = o + _dot_nt(pp.reshape(rows, page).astype(BF), v_t)
    l = jnp.sum(psum, axis=2, keepdims=True)
    o = o.reshape(dec_seq, N_HEADS, d_b) / l
    o_ref[:, 0, :] = jnp.sum(jnp.where(head_mask[None], o, 0.0), axis=1).astype(BF)


def _decode(page_table_flat, q_s, k_s, v_s, lfn_t, cache_kt, cache_vt, cache_lf_t, *,
            layer, dec_batch, dec_seq):
    n_pages = page_table_flat.shape[0] // dec_batch
    page = cache_kt.shape[4]
    d_b = N_HEADS * HEAD_DIM
    tok_spec = pl.BlockSpec((dec_seq, None, 1, d_b), lambda b, pt: (0, b, 0, 0))

    def page_map(p, tail, b, pt):
        return (layer, pt[b * n_pages + p]) + (0,) * tail

    kv_specs = [pl.BlockSpec((None, None, N_HEADS, HEAD_DIM, page),
                             functools.partial(page_map, p, 3)) for p in range(n_pages)]
    lf_specs = [pl.BlockSpec((None, None, N_HEADS, page),
                             functools.partial(page_map, p, 2)) for p in range(n_pages)]
    kernel = functools.partial(_decode_kernel, n_pages=n_pages, dec_seq=dec_seq, page=page)
    grid_spec = pltpu.PrefetchScalarGridSpec(
        num_scalar_prefetch=1,
        grid=(dec_batch,),
        in_specs=[tok_spec, tok_spec, tok_spec,
                  pl.BlockSpec((None, N_HEADS, page), lambda b, pt: (b, 0, 0))]
        + kv_specs + kv_specs + lf_specs,
        out_specs=tok_spec,
        scratch_shapes=[pltpu.VMEM((page, d_b), F32), pltpu.VMEM((page, d_b), F32)],
    )
    return pl.pallas_call(
        kernel,
        grid_spec=grid_spec,
        out_shape=jax.ShapeDtypeStruct((dec_seq, dec_batch, 1, d_b), BF),
        compiler_params=pltpu.CompilerParams(
            dimension_semantics=("arbitrary",), vmem_limit_bytes=VMEM_LIMIT),
        name="decode_attn",
    )(page_table_flat, q_s, k_s, v_s, lfn_t,
      *([cache_kt] * n_pages), *([cache_vt] * n_pages), *([cache_lf_t] * n_pages))


def _even_out_kernel(xp_ref, xs_ref, ap_ref, as_ref, tp_ref, ts_ref, woa_ref, wob_ref,
                     g_ref, wu_ref, wd_ref, yp_ref, ys_ref, *, n_ptiles):
    i = pl.program_id(0)
    is_s = i >= n_ptiles
    x = jnp.where(is_s, xs_ref[...], xp_ref[...])
    a = jnp.where(is_s, as_ref[...], ap_ref[...])
    t = jnp.where(is_s, ts_ref[...], tp_ref[...])
    x = x + (_dot(a, woa_ref[...]) + _dot(t, wob_ref[...]))
    y = _ffn(x, g_ref, wu_ref, wd_ref)

    @pl.when(is_s)
    def _():
        ys_ref[...] = y

    @pl.when(jnp.logical_not(is_s))
    def _():
        yp_ref[...] = y


def _even_out(xp, xs, a_p, a_s, t_p, t_s, wo_all, g, wu_all, wd_all, *, e, layer):
    n_p, d = xp.shape
    d_a = a_p.shape[1]
    d_b = t_p.shape[1]
    n_ptiles = n_p // TM
    last_p = n_ptiles - 1
    p_row = lambda i: (jnp.minimum(i, last_p), 0)
    kernel = functools.partial(_even_out_kernel, n_ptiles=n_ptiles)
    assert d_a == d_b
    return pl.pallas_call(
        kernel,
        grid=(n_ptiles + 1,),
        in_specs=[
            pl.BlockSpec((TM, d), p_row), _resident(xs.shape),
            pl.BlockSpec((TM, d_a), p_row), _resident(a_s.shape),
            pl.BlockSpec((TM, d_b), p_row), _resident(t_s.shape),
            _layer_slab(wo_all, e, (d_a, d), (0, 0)), _layer_slab(wo_all, e, (d_b, d), (1, 0)),
            _resident(g.shape), _layer_slab(wu_all, layer), _layer_slab(wd_all, layer),
        ],
        out_specs=(pl.BlockSpec((TM, d), p_row), pl.BlockSpec((TM, d), lambda i: (0, 0))),
        out_shape=(jax.ShapeDtypeStruct(xp.shape, F32), jax.ShapeDtypeStruct(xs.shape, F32)),
        compiler_params=pltpu.CompilerParams(
            dimension_semantics=("arbitrary",), vmem_limit_bytes=VMEM_LIMIT),
        name="even_out",
    )(xp, xs, a_p, a_s, t_p, t_s, wo_all, wo_all, g, wu_all, wd_all)


def _odd_kernel(xp_ref, xs_ref, gm_ref, wi_ref, lg_ref, lb_ref, ws_ref, bs_ref, wsm_ref,
                wo_ref, gf_ref, wu_ref, wd_ref, yp_ref, ys_ref, vs_ref, mix_ref,
                *, n_ptiles, dec_batch, dec_seq):
    i = pl.program_id(0)
    is_s = i >= n_ptiles
    d_c = wo_ref.shape[0]
    n_groups = ws_ref.shape[0]
    c_hd = d_c // n_groups

    x = jnp.where(is_s, xs_ref[...], xp_ref[...])
    h = _rms(x, gm_ref[...]).astype(BF)
    u = jax.nn.gelu(_dot(h, wi_ref[:, :d_c]))
    v = _ln(jax.nn.gelu(_dot(h, wi_ref[:, d_c:])), lg_ref[...], lb_ref[...])
    v_b = v.astype(BF)

    r = lax.broadcasted_iota(jnp.int32, (CHUNK, CHUNK), 0)
    c = lax.broadcasted_iota(jnp.int32, (CHUNK, CHUNK), 1)
    for hg in range(n_groups):
        w = jnp.where(r >= c, ws_ref[hg], jnp.zeros((CHUNK, CHUNK), BF))
        cols = slice(hg * c_hd, (hg + 1) * c_hd)
        for cb in range(TM // CHUNK):
            rows = slice(cb * CHUNK, (cb + 1) * CHUNK)
            mix_ref[rows, cols] = _dot(w, v_b[rows, cols]) + bs_ref[:, cols]

    v_r = v_b.astype(F32)
    mix_s = []
    for t in range(dec_seq):
        acc = jnp.zeros((dec_batch, d_c), F32) + bs_ref[t:t + 1, :]
        for j in range(t + 1):
            wt = wsm_ref[t * dec_seq + j:t * dec_seq + j + 1, :].astype(BF).astype(F32)
            acc = acc + wt * v_r[j * dec_batch:(j + 1) * dec_batch, :]
        mix_s.append(acc)
    mix = jnp.where(is_s, jnp.concatenate(mix_s, axis=0), mix_ref[...])

    gte = (u * mix).astype(BF)
    x = x + _dot(gte, wo_ref[...])
    y = _ffn(x, gf_ref, wu_ref, wd_ref)

    @pl.when(is_s)
    def _():
        ys_ref[...] = y
        vs_ref[...] = v

    @pl.when(jnp.logical_not(is_s))
    def _():
        yp_ref[...] = y


def _odd(xp, xs, gm, wi_all, lg, lb, ws_all, bs_full, ws_small, wo_all, gf, wu_all, wd_all, *,
         o, layer, dec_batch, dec_seq):
    n_p, d = xp.shape
    d_c = wo_all.shape[1]
    n_ptiles = n_p // TM
    last_p = n_ptiles - 1
    p_row = lambda i: (jnp.minimum(i, last_p), 0)
    s_row = lambda i: (0, 0)
    kernel = functools.partial(_odd_kernel, n_ptiles=n_ptiles, dec_batch=dec_batch,
                               dec_seq=dec_seq)
    return pl.pallas_call(
        kernel,
        grid=(n_ptiles + 1,),
        in_specs=[
            pl.BlockSpec((TM, d), p_row), _resident(xs.shape),
            _resident(gm.shape), _layer_slab(wi_all, o), _resident(lg.shape), _resident(lb.shape),
            _layer_slab(ws_all, o), _resident(bs_full.shape), _resident(ws_small.shape),
            _layer_slab(wo_all, o), _resident(gf.shape),
            _layer_slab(wu_all, layer), _layer_slab(wd_all, layer),
        ],
        out_specs=(pl.BlockSpec((TM, d), p_row), pl.BlockSpec((TM, d), s_row),
                   pl.BlockSpec((TM, d_c), s_row)),
        out_shape=(jax.ShapeDtypeStruct(xp.shape, F32), jax.ShapeDtypeStruct(xs.shape, F32),
                   jax.ShapeDtypeStruct((TM, d_c), F32)),
        scratch_shapes=[pltpu.VMEM((TM, d_c), F32)],
        compiler_params=pltpu.CompilerParams(
            dimension_semantics=("arbitrary",), vmem_limit_bytes=VMEM_LIMIT),
        name="odd_layer",
    )(xp, xs, gm, wi_all, lg, lb, ws_all, bs_full, ws_small, wo_all, gf, wu_all, wd_all)


def kernel(x_prompt, x_sample, cache_k, cache_v, cache_logf, state_conv, page_table,
           norm_mix_g, norm_ffn_g, w_in_ab, b_forget, q_norm_g, k_norm_g,
           conv_w, conv_b, conv_ln_g, conv_ln_b, w_out_ab,
           w_in_c, sgu_ln_g, sgu_ln_b, w_spatial, b_spatial, w_out_c,
           w_ff_up, w_ff_down):
    batch, seq, d = x_prompt.shape
    dec_batch, dec_seq, _ = x_sample.shape
    depth = norm_mix_g.shape[0]
    n_ab, n_pool, page, n_heads, head_dim = cache_k.shape
    d_b = n_heads * head_dim
    d_a = conv_w.shape[2]
    n_groups = w_spatial.shape[1]
    d_c = w_out_c.shape[1]
    assert dec_batch * dec_seq == TM and seq % TM == 0 and page == LANES
    assert n_heads == N_HEADS and head_dim == HEAD_DIM and conv_w.shape[1] == CONV_WIDTH
    assert w_spatial.shape[2] == CHUNK and dec_seq <= SUBLANES

    xp = x_prompt.reshape(batch * seq, d)
    xs = jnp.swapaxes(x_sample, 0, 1).reshape(TM, d)

    row = lambda a: a.reshape(1, -1).astype(F32)
    ck_t = jnp.transpose(cache_k, (0, 1, 3, 4, 2))
    cv_t = jnp.transpose(cache_v, (0, 1, 3, 4, 2))
    clf_t = jnp.swapaxes(cache_logf, 2, 3)
    hist_t = jnp.swapaxes(state_conv, 1, 2)
    pt_flat = page_table.reshape(-1)
    head_of = jnp.arange(d_b) // head_dim
    bd = (head_of[:, None] == head_of[None, :]).astype(BF) * (1.0 / head_dim)
    src = jnp.arange(LANES)
    dst = (src % n_heads) * LANES + head_dim + src // n_heads
    sel = jnp.logical_and(src[:, None] < 3 * n_heads,
                          jnp.arange(n_heads * LANES)[None, :] == dst[:, None]).astype(BF)

    wu_all = w_ff_up.astype(BF)
    wd_all = w_ff_down.astype(BF)
    w_in_all = jnp.pad(w_in_ab, ((0, 0), (0, 0), (0, LANES - n_heads))).astype(BF)
    wo_ab_all = w_out_ab.astype(BF)
    wi_c_all = w_in_c.astype(BF)
    wo_c_all = w_out_c.astype(BF)
    ws_all = w_spatial.astype(BF)

    outs = {k: [] for k in ("cp", "ks", "vs", "fs", "cs", "chs")}
    stacked = (jnp.zeros((n_ab, batch, d_b, seq), F32), jnp.zeros((n_ab, batch, d_b, seq), F32),
               jnp.zeros((n_ab, batch, n_heads, seq), F32))
    for layer in range(depth):
        g_mix = row(norm_mix_g[layer])
        g_ffn = row(norm_ffn_g[layer])
        if layer % 2 == 0:
            e = layer // 2
            bfg = jnp.pad(b_forget[e], (0, LANES - n_heads)).reshape(1, LANES)
            qg = row(jnp.tile(q_norm_g[e], n_heads))
            kg = row(jnp.tile(k_norm_g[e], n_heads))
            cw, cb = conv_w[e], row(conv_b[e])
            lg, lb = row(conv_ln_g[e]), row(conv_ln_b[e])
            (a_p, q_t, k_aug, v_t, kt_all, vt_all, lft_all, hist_p,
             q_s, k_s, v_s, glu_s, lf_s) = _even_in(
                xp, xs, g_mix, w_in_all, bfg, qg, kg, bd, cw, cb, lg, lb, sel, stacked,
                layer=e, n_layers=n_ab, batch=batch, seq=seq)
            stacked = (kt_all, vt_all, lft_all)

            a_s = _sample_conv(glu_s, hist_t[e], cw, cb, lg, lb,
                               dec_batch=dec_batch, dec_seq=dec_seq)

            t_p = _flash(q_t, k_aug, v_t, batch=batch, seq=seq)

            lf_s4 = lf_s[:, :n_heads].reshape(dec_seq, dec_batch, n_heads)
            lfn_t = jnp.pad(jnp.transpose(lf_s4, (1, 2, 0)),
                            ((0, 0), (0, 0), (0, page - dec_seq)))
            tok = lambda a: a.reshape(dec_seq, dec_batch, 1, d_b)
            t_s = _decode(pt_flat, tok(q_s), tok(k_s), tok(v_s), lfn_t, ck_t, cv_t, clf_t,
                          layer=e, dec_batch=dec_batch, dec_seq=dec_seq).reshape(TM, d_b)

            xp, xs = _even_out(xp, xs, a_p, a_s, t_p, t_s, wo_ab_all, g_ffn, wu_all, wd_all,
                               e=e, layer=layer)

            sample_major = lambda a: jnp.swapaxes(a.reshape(dec_seq, dec_batch, -1), 0, 1)
            outs["cp"].append(hist_p)
            outs["ks"].append(sample_major(k_s).reshape(dec_batch, dec_seq, n_heads, head_dim))
            outs["vs"].append(sample_major(v_s).reshape(dec_batch, dec_seq, n_heads, head_dim))
            outs["fs"].append(sample_major(lf_s[:, :n_heads]))
            outs["cs"].append(jnp.swapaxes(jnp.concatenate(
                [hist_t[e][dec_seq:], glu_s.reshape(dec_seq, dec_batch, d_a)], axis=0), 0, 1))
        else:
            o = layer // 2
            bs_full = jnp.repeat(b_spatial[o].T, d_c // n_groups, axis=1)
            ws_small = jnp.repeat(
                jnp.transpose(w_spatial[o][:, :dec_seq, :dec_seq], (1, 2, 0)),
                d_c // n_groups, axis=2).reshape(dec_seq * dec_seq, d_c)
            xp, xs, v_rows = _odd(
                xp, xs, g_mix, wi_c_all, row(sgu_ln_g[o]), row(sgu_ln_b[o]),
                ws_all, bs_full, ws_small, wo_c_all, g_ffn, wu_all, wd_all,
                o=o, layer=layer, dec_batch=dec_batch, dec_seq=dec_seq)
            outs["chs"].append(jnp.swapaxes(v_rows.reshape(dec_seq, dec_batch, d_c), 0, 1))

    y_prompt = xp.reshape(batch, seq, d)
    y_sample = jnp.swapaxes(xs.reshape(dec_seq, dec_batch, d), 0, 1)
    kt_all, vt_all, lft_all = stacked
    to_seq_major = lambda a: jnp.transpose(
        a.reshape(n_ab, batch, n_heads, head_dim, seq), (0, 1, 4, 2, 3))
    return (y_prompt, y_sample,
            to_seq_major(kt_all), to_seq_major(vt_all), jnp.swapaxes(lft_all, 2, 3),
            jnp.stack(outs["ks"]), jnp.stack(outs["vs"]), jnp.stack(outs["fs"]),
            jnp.stack(outs["cp"]), jnp.stack(outs["cs"]), jnp.stack(outs["chs"]))
```
